```python
import math
import jax
import jax.numpy as jnp
from jax import lax
import numpy as np

D_MODEL = 1024
BATCH = 16
SEQ = 2048
DEPTH = 1
DEC_BATCH = 4
DEC_SEQ = 4096
PAST_LEN = 128

D_FF = 2816
CHUNK = 128
A_WIDTH = 512
A_GROUPS = 8
A_GROUP_DIM = A_WIDTH // A_GROUPS
B_HEADS = 4
HEAD_DIM = 64
B_QK = B_HEADS * 2 * HEAD_DIM
B_V = B_HEADS * 2 * HEAD_DIM
Q_BLOCK = 128
N_BUCKETS = 32
MAX_DISTANCE = 128
IN_COLS = 2 * A_WIDTH + 2 * B_QK + B_V + 2 * D_MODEL
EPS = 1e-6

kernel_name = "hybrid_sgu_diffattn_macaron_encoder"


def rms_norm(x, g):
    xf = x.astype(jnp.float32)
    y = xf * lax.rsqrt(jnp.mean(xf * xf, axis=-1, keepdims=True) + EPS)
    return (y * g.astype(jnp.float32)).astype(x.dtype)


def swiglu_ffn(x, w_in, w_out):
    gate, up = jnp.split(x @ w_in, 2, axis=-1)
    return (jax.nn.silu(gate) * up) @ w_out


def rel_position_bucket(rel):
    half = N_BUCKETS // 2
    max_exact = half // 2
    bucket = jnp.where(rel > 0, half, 0).astype(jnp.int32)
    n = jnp.abs(rel)
    nf = jnp.maximum(n, 1).astype(jnp.float32)
    large = max_exact + (jnp.log(nf / max_exact) / math.log(MAX_DISTANCE / max_exact)
                         * (half - max_exact)).astype(jnp.int32)
    large = jnp.minimum(large, half - 1)
    return bucket + jnp.where(n < max_exact, n, large)


def spatial_gating(u, v, norm_g, w_s, b_s):
    B, S, _ = v.shape
    vc = rms_norm(v, norm_g).reshape(B, S // CHUNK, CHUNK, A_GROUPS, A_GROUP_DIM)
    s = jnp.einsum('gts,bcsgd->bctgd', w_s, vc) + b_s.T[None, None, :, :, None]
    return u * s.reshape(B, S, A_WIDTH)


def diff_attention(q, k, v, lam, rel_bias):
    B, S, H, _, dh = q.shape
    nq = S // Q_BLOCK
    qb = jnp.moveaxis(q.reshape(B, nq, Q_BLOCK, H, 2, dh), 1, 0)
    kpos = jnp.arange(S, dtype=jnp.int32)

    def one_block(args):
        qblk, start = args
        logits = jnp.einsum('bqhcd,bkhcd->bhcqk', qblk, k,
                            preferred_element_type=jnp.float32)
        qpos = start + jnp.arange(Q_BLOCK, dtype=jnp.int32)
        bias = rel_bias[rel_position_bucket(kpos[None, :] - qpos[:, None])].astype(jnp.float32)
        logits = logits + jnp.transpose(bias, (2, 0, 1))[None, :, None]
        p = jax.nn.softmax(logits, axis=-1)
        w = p[:, :, 0] - lam * p[:, :, 1]
        return jnp.einsum('bhqk,bkhe->bqhe', w.astype(v.dtype), v)

    starts = jnp.arange(nq, dtype=jnp.int32) * Q_BLOCK
    out = lax.map(one_block, (qb, starts))
    return jnp.moveaxis(out, 0, 1).reshape(B, S, H, 2 * dh)


def encoder_layer(x, layer_idx, rel_bias, ffn1_norm, ffn1_w_in, ffn1_w_out, mix_norm, w_in,
                  gate_bias, sgu_norm, sgu_w, sgu_b, q_norm, k_norm, lambda_q1, lambda_k1,
                  lambda_q2, lambda_k2, diff_subln, w_proj_a, w_proj_b, w_out,
                  ffn2_norm, ffn2_w_in, ffn2_w_out, final_norm):
    B, S, _ = x.shape
    x = x + 0.5 * swiglu_ffn(rms_norm(x, ffn1_norm), ffn1_w_in, ffn1_w_out)

    h = rms_norm(x, mix_norm)
    proj = h @ w_in
    c0 = 2 * A_WIDTH
    c1 = c0 + B_QK
    c2 = c1 + B_QK
    c3 = c2 + B_V
    uv, q, k, v, g = jnp.split(proj, [c0, c1, c2, c3], axis=-1)

    u, va = jnp.split(jax.nn.gelu(uv), 2, axis=-1)
    a_out = spatial_gating(u, va, sgu_norm, sgu_w, sgu_b)

    q = rms_norm(q.reshape(B, S, B_HEADS, 2, HEAD_DIM), q_norm) * (HEAD_DIM ** -0.5)
    k = rms_norm(k.reshape(B, S, B_HEADS, 2, HEAD_DIM), k_norm)
    v = v.reshape(B, S, B_HEADS, 2 * HEAD_DIM)
    lambda_init = 0.8 - 0.6 * math.exp(-0.3 * layer_idx)
    lam = (jnp.exp(jnp.sum(lambda_q1.astype(jnp.float32) * lambda_k1.astype(jnp.float32)))
           - jnp.exp(jnp.sum(lambda_q2.astype(jnp.float32) * lambda_k2.astype(jnp.float32)))
           + lambda_init)
    b_out = diff_attention(q, k, v, lam, rel_bias)
    b_out = (rms_norm(b_out, diff_subln) * (1.0 - lambda_init)).reshape(B, S, B_V)

    gates = jax.nn.sigmoid((g + gate_bias).astype(jnp.float32)).astype(x.dtype)
    g_a, g_b = jnp.split(gates, 2, axis=-1)
    merged = g_a * (a_out @ w_proj_a) + g_b * (b_out @ w_proj_b)
    x = x + merged @ w_out

    x = x + 0.5 * swiglu_ffn(rms_norm(x, ffn2_norm), ffn2_w_in, ffn2_w_out)
    return rms_norm(x, final_norm)


def setup_inputs(seed: int = 0) -> dict:
    key = jax.random.key(seed)
    ks = iter(jax.random.split(key, 40))
    nrm = lambda shape, s: jax.random.normal(next(ks), shape, jnp.float32) * s
    gain = lambda shape: 1.0 + nrm(shape, 0.01)
    L = DEPTH
    return {
        "x_prompt": nrm((BATCH, SEQ, D_MODEL), 1.0),
        "x_sample": nrm((DEC_BATCH, DEC_SEQ, D_MODEL), 1.0),
        "rel_bias": nrm((N_BUCKETS, B_HEADS), 0.5),
        "ffn1_norm": gain((L, D_MODEL)),
        "ffn1_w_in": nrm((L, D_MODEL, 2 * D_FF), D_MODEL ** -0.5),
        "ffn1_w_out": nrm((L, D_FF, D_MODEL), D_FF ** -0.5),
        "mix_norm": gain((L, D_MODEL)),
        "w_in": nrm((L, D_MODEL, IN_COLS), D_MODEL ** -0.5),
        "gate_bias": nrm((L, 2 * D_MODEL), 0.02),
        "sgu_norm": gain((L, A_WIDTH)),
        "sgu_w": nrm((L, A_GROUPS, CHUNK, CHUNK), CHUNK ** -0.5),
        "sgu_b": 1.0 + nrm((L, A_GROUPS, CHUNK), 0.02),
        "q_norm": gain((L, HEAD_DIM)),
        "k_norm": gain((L, HEAD_DIM)),
        "lambda_q1": nrm((L, HEAD_DIM), 0.1),
        "lambda_k1": nrm((L, HEAD_DIM), 0.1),
        "lambda_q2": nrm((L, HEAD_DIM), 0.1),
        "lambda_k2": nrm((L, HEAD_DIM), 0.1),
        "diff_subln": gain((L, 2 * HEAD_DIM)),
        "w_proj_a": nrm((L, A_WIDTH, D_MODEL), A_WIDTH ** -0.5),
        "w_proj_b": nrm((L, B_V, D_MODEL), B_V ** -0.5),
        "w_out": nrm((L, D_MODEL, D_MODEL), D_MODEL ** -0.5),
        "ffn2_norm": gain((L, D_MODEL)),
        "ffn2_w_in": nrm((L, D_MODEL, 2 * D_FF), D_MODEL ** -0.5),
        "ffn2_w_out": nrm((L, D_FF, D_MODEL), D_FF ** -0.5),
        "final_norm": gain((L, D_MODEL)),
    }


def reference(x_prompt, x_sample, rel_bias, ffn1_norm, ffn1_w_in, ffn1_w_out, mix_norm, w_in,
              gate_bias, sgu_norm, sgu_w, sgu_b, q_norm, k_norm, lambda_q1, lambda_k1,
              lambda_q2, lambda_k2, diff_subln, w_proj_a, w_proj_b, w_out,
              ffn2_norm, ffn2_w_in, ffn2_w_out, final_norm):
    y_prompt = x_prompt
    y_sample = x_sample
    for l in range(DEPTH):
        layer_params = (ffn1_norm[l], ffn1_w_in[l], ffn1_w_out[l], mix_norm[l], w_in[l],
                        gate_bias[l], sgu_norm[l], sgu_w[l], sgu_b[l], q_norm[l], k_norm[l],
                        lambda_q1[l], lambda_k1[l], lambda_q2[l], lambda_k2[l], diff_subln[l],
                        w_proj_a[l], w_proj_b[l], w_out[l], ffn2_norm[l], ffn2_w_in[l],
                        ffn2_w_out[l], final_norm[l])
        y_prompt = encoder_layer(y_prompt, l, rel_bias, *layer_params)
        y_sample = encoder_layer(y_sample, l, rel_bias, *layer_params)
    return (y_prompt, y_sample)
```

```python
import functools
import math

import jax
import jax.numpy as jnp
from jax import lax
from jax.experimental import pallas as pl
from jax.experimental.pallas import tpu as pltpu

D_MODEL = 1024
D_FF = 2816
CHUNK = 128
A_WIDTH = 512
A_GROUPS = 8
A_GROUP_DIM = A_WIDTH // A_GROUPS
B_HEADS = 4
HEAD_DIM = 64
HEAD_WIDTH = 2 * HEAD_DIM
B_QK = B_HEADS * HEAD_WIDTH
B_V = B_HEADS * HEAD_WIDTH
N_BUCKETS = 32
MAX_DISTANCE = 128
IN_COLS = 2 * A_WIDTH + 2 * B_QK + B_V + 2 * D_MODEL
EPS = 1e-6

V7X_LANES = 128
V7X_VMEM_LIMIT_BYTES = 56 * 1024 * 1024

ROW_TILE = 256
ATTN_TILE = 256

F32 = jnp.float32
BF16 = jnp.bfloat16


def _rms(x, g):
    return x * lax.rsqrt(jnp.mean(x * x, axis=-1, keepdims=True) + EPS) * g


def _const_spec(shape):
    return pl.BlockSpec(shape, lambda *_: (0,) * len(shape), pipeline_mode=pl.Buffered(1))


def _params(n_axes):
    return pltpu.CompilerParams(dimension_semantics=("parallel",) * n_axes,
                                vmem_limit_bytes=V7X_VMEM_LIMIT_BYTES)


def _ffn_body(x_ref, g_ref, win_ref, wout_ref, *rest, final_norm):
    o_ref = rest[-1]
    x = x_ref[...]
    h = _rms(x, g_ref[...]).astype(BF16)
    gu = jnp.dot(h, win_ref[...], preferred_element_type=F32)
    gate = gu[:, :D_FF]
    up = gu[:, D_FF:]
    act = (gate * jax.nn.sigmoid(gate) * up).astype(BF16)
    y = x + 0.5 * jnp.dot(act, wout_ref[...], preferred_element_type=F32)
    if final_norm:
        y = _rms(y, rest[0][...])
    o_ref[...] = y


def _ffn(x, g, w_in, w_out, final_g=None):
    n = x.shape[0]
    row = pl.BlockSpec((ROW_TILE, D_MODEL), lambda i: (i, 0))
    in_specs = [row, _const_spec((1, D_MODEL)), _const_spec((D_MODEL, 2 * D_FF)),
                _const_spec((D_FF, D_MODEL))]
    args = [x, g, w_in, w_out]
    if final_g is not None:
        in_specs.append(_const_spec((1, D_MODEL)))
        args.append(final_g)
    return pl.pallas_call(
        functools.partial(_ffn_body, final_norm=final_g is not None),
        grid=(n // ROW_TILE,),
        in_specs=in_specs,
        out_specs=row,
        out_shape=jax.ShapeDtypeStruct((n, D_MODEL), F32),
        compiler_params=_params(1),
        name="ffn_final" if final_g is not None else "ffn",
    )(*args)


def _proj_body(x_ref, g_ref, win_ref, gbias_ref, sgug_ref, wcat_ref, sbias_ref, seg_ref,
               qg_ref, kg_ref, a_ref, q_ref, k_ref, v_ref, gate_ref):
    tm = x_ref.shape[0]
    h = _rms(x_ref[...], g_ref[...]).astype(BF16)
    proj = jnp.dot(h, win_ref[...], preferred_element_type=F32)
    c0 = 2 * A_WIDTH
    c1 = c0 + B_QK
    c2 = c1 + B_QK
    c3 = c2 + B_V

    uv = jax.nn.gelu(proj[:, :c0])
    u = uv[:, :A_WIDTH]
    vn = _rms(uv[:, A_WIDTH:], sgug_ref[...])
    lane = lax.broadcasted_iota(jnp.int32, (tm, A_WIDTH), 1)
    low = (lane & (V7X_LANES - 1)) < A_GROUP_DIM
    v_low = jnp.where(low, vn, 0.0).astype(BF16)
    v_high = jnp.where(low, 0.0, vn).astype(BF16)
    for c in range(tm // CHUNK):
        rows = slice(c * CHUNK, (c + 1) * CHUNK)
        slabs = []
        for p in range(A_WIDTH // V7X_LANES):
            cols = slice(p * V7X_LANES, (p + 1) * V7X_LANES)
            rhs = jnp.concatenate([v_low[rows, cols], v_high[rows, cols]], axis=0)
            slabs.append(jnp.dot(wcat_ref[p], rhs, preferred_element_type=F32))
        s = jnp.concatenate(slabs, axis=1) + sbias_ref[...]
        a_ref[rows, :] = (u[rows, :] * s).astype(a_ref.dtype)

    def seg_norm(t, gain):
        ms = jnp.dot((t * t).astype(BF16), seg_ref[...], preferred_element_type=F32)
        return (t * lax.rsqrt(ms + EPS) * gain).astype(BF16)

    q_ref[...] = seg_norm(proj[:, c0:c1], qg_ref[...])
    k_ref[...] = seg_norm(proj[:, c1:c2], kg_ref[...])
    v_ref[...] = proj[:, c2:c3].astype(BF16)
    gate_ref[...] = jax.nn.sigmoid(proj[:, c3:] + gbias_ref[...]).astype(gate_ref.dtype)


def _proj(x, g, w_in, gate_bias, sgu_g, wcat, sbias, seg, qg, kg):
    n = x.shape[0]
    row = lambda w: pl.BlockSpec((ROW_TILE, w), lambda i: (i, 0))
    out_w = (A_WIDTH, B_QK, B_QK, B_V, 2 * D_MODEL)
    return pl.pallas_call(
        _proj_body,
        grid=(n // ROW_TILE,),
        in_specs=[row(D_MODEL), _const_spec((1, D_MODEL)), _const_spec((D_MODEL, IN_COLS)),
                  _const_spec((1, 2 * D_MODEL)), _const_spec((1, A_WIDTH)),
                  _const_spec(wcat.shape), _const_spec(sbias.shape), _const_spec(seg.shape),
                  _const_spec((1, B_QK)), _const_spec((1, B_QK))],
        out_specs=[row(w) for w in out_w],
        out_shape=[jax.ShapeDtypeStruct((n, w), BF16) for w in out_w],
        compiler_params=_params(1),
        name="proj",
    )(x, g, w_in, gate_bias, sgu_g, wcat, sbias, seg, qg, kg)


def _bucket_thresholds():
    half = N_BUCKETS // 2
    max_exact = half // 2
    n_log = half - max_exact
    ratio = MAX_DISTANCE // max_exact
    out = []
    for j in range(1, n_log):
        target = max_exact ** n_log * ratio ** j
        n = max_exact
        while n ** n_log < target:
            n += 1
        out.append(n)
    return max_exact, out


def _prep_body(rb_ref, lq1_ref, lk1_ref, lq2_ref, lk2_ref, tiles_ref, lam_ref, *, lambda_init):
    t = ATTN_TILE
    half = N_BUCKETS // 2
    max_exact, thresholds = _bucket_thresholds()
    row = lax.broadcasted_iota(jnp.int32, (t, t), 0)
    col = lax.broadcasted_iota(jnp.int32, (t, t), 1)
    for d in (-1, 0, 1):
        rel = col - row + d * t
        n = jnp.abs(rel)
        large = jnp.full((t, t), max_exact, jnp.int32)
        for th in thresholds:
            large = large + jnp.where(n >= th, 1, 0)
        bucket = jnp.where(rel > 0, half, 0) + jnp.where(n < max_exact, n, large)
        for hd in range(B_HEADS):
            tile = jnp.zeros((t, t), F32)
            for b in range(N_BUCKETS):
                tile = jnp.where(bucket == b, rb_ref[b, hd], tile)
            tiles_ref[hd, d + 1] = tile
    s1 = jnp.sum(lq1_ref[...] * lk1_ref[...], axis=-1, keepdims=True)
    s2 = jnp.sum(lq2_ref[...] * lk2_ref[...], axis=-1, keepdims=True)
    lam = jnp.exp(s1) - jnp.exp(s2) + lambda_init
    lam_ref[...] = jnp.broadcast_to(lam, lam_ref.shape)


def _prep(rel_bias, lq1, lk1, lq2, lk2, lambda_init):
    t = ATTN_TILE
    vec = pl.BlockSpec((1, HEAD_DIM), lambda: (0, 0))
    return pl.pallas_call(
        functools.partial(_prep_body, lambda_init=lambda_init),
        in_specs=[pl.BlockSpec(memory_space=pltpu.SMEM), vec, vec, vec, vec],
        out_specs=[pl.BlockSpec((B_HEADS, 3, t, t), lambda: (0, 0, 0, 0)),
                   pl.BlockSpec((8, V7X_LANES), lambda: (0, 0))],
        out_shape=[jax.ShapeDtypeStruct((B_HEADS, 3, t, t), F32),
                   jax.ShapeDtypeStruct((8, V7X_LANES), F32)],
        name="prep",
    )(rel_bias, lq1, lk1, lq2, lk2)


def _attn_body(rb_ref, lam_ref, q_ref, k_ref, v_ref, tiles_ref, subg_ref, o_ref,
               m0_ref, l0_ref, acc0_ref, m1_ref, l1_ref, acc1_ref, *, out_scale):
    t = ATTN_TILE
    nk = k_ref.shape[0] // t
    hd = pl.program_id(1)
    i = pl.program_id(2)
    half = N_BUCKETS // 2
    far_left = rb_ref[half - 1, hd]
    far_right = rb_ref[N_BUCKETS - 1, hd]

    lane = lax.broadcasted_iota(jnp.int32, (t, HEAD_WIDTH), 1)
    first = jnp.where(lane < HEAD_DIM, 1.0, 0.0).astype(BF16)
    q = q_ref[...]
    q_comp = (q * first, q * (1.0 - first))
    state = ((m0_ref, l0_ref, acc0_ref), (m1_ref, l1_ref, acc1_ref))
    reps = t // V7X_LANES

    def block(j, bias, init):
        start = pl.multiple_of(j * t, t)
        kj = k_ref[pl.ds(start, t), :]
        vj = v_ref[pl.ds(start, t), :]
        for qc, (m_ref, l_ref, acc_ref) in zip(q_comp, state):
            s = lax.dot_general(qc, kj, (((1,), (1,)), ((), ())), preferred_element_type=F32)
            s = s + bias
            m_cur = jnp.max(s, axis=1, keepdims=True)
            if init:
                m_next = jnp.broadcast_to(m_cur, (t, V7X_LANES))
            else:
                m_prev = m_ref[...]
                m_next = jnp.maximum(m_prev, m_cur)
            p = jnp.exp(s - jnp.concatenate([m_next] * reps, axis=1))
            l_cur = jnp.sum(p, axis=1, keepdims=True)
            pv = jnp.dot(p.astype(BF16), vj, preferred_element_type=F32)
            if init:
                l_ref[...] = jnp.broadcast_to(l_cur, (t, V7X_LANES))
                acc_ref[...] = pv
            else:
                alpha = jnp.exp(m_prev - m_next)
                l_ref[...] = alpha * l_ref[...] + l_cur
                acc_ref[...] = alpha * acc_ref[...] + pv
            m_ref[...] = m_next

    block(i, tiles_ref[1], True)

    @pl.when(i > 0)
    def _():
        block(i - 1, tiles_ref[0], False)

    @pl.when(i < nk - 1)
    def _():
        block(i + 1, tiles_ref[2], False)

    lax.fori_loop(0, jnp.maximum(i - 1, 0), lambda j, c: (block(j, far_left, False), c)[1], 0)
    lax.fori_loop(i + 2, nk, lambda j, c: (block(j, far_right, False), c)[1], 0)

    lam = lam_ref[0:1, 0:1]
    out = acc0_ref[...] / l0_ref[...] - lam * (acc1_ref[...] / l1_ref[...])
    o_ref[...] = (_rms(out, subg_ref[...]) * out_scale).astype(o_ref.dtype)


def _attn(rel_bias, lam, q, k, v, tiles, subg, out_scale):
    b, s, _ = q.shape
    t = ATTN_TILE
    qspec = pl.BlockSpec((None, t, HEAD_WIDTH), lambda bi, hi, qi: (bi, qi, hi))
    kvspec = pl.BlockSpec((None, s, HEAD_WIDTH), lambda bi, hi, qi: (bi, 0, hi))
    return pl.pallas_call(
        functools.partial(_attn_body, out_scale=out_scale),
        grid=(b, B_HEADS, s // t),
        in_specs=[pl.BlockSpec(memory_space=pltpu.SMEM),
                  pl.BlockSpec((8, V7X_LANES), lambda bi, hi, qi: (0, 0)),
                  qspec, kvspec, kvspec,
                  pl.BlockSpec((None, 3, t, t), lambda bi, hi, qi: (hi, 0, 0, 0)),
                  pl.BlockSpec((1, HEAD_WIDTH), lambda bi, hi, qi: (0, 0))],
        out_specs=qspec,
        out_shape=jax.ShapeDtypeStruct((b, s, B_V), BF16),
        scratch_shapes=[pltpu.VMEM((t, V7X_LANES), F32)] * 6,
        compiler_params=_params(3),
        name="attn",
    )(rel_bias, lam, q, k, v, tiles, subg)


def _merge_body(x_ref, a_ref, b_ref, gate_ref, wpa_ref, wpb_ref, wo_ref, o_ref):
    ga = gate_ref[:, :D_MODEL].astype(F32)
    gb = gate_ref[:, D_MODEL:].astype(F32)
    merged = (ga * jnp.dot(a_ref[...], wpa_ref[...], preferred_element_type=F32)
              + gb * jnp.dot(b_ref[...], wpb_ref[...], preferred_element_type=F32))
    o_ref[...] = x_ref[...] + jnp.dot(merged.astype(BF16), wo_ref[...],
                                      preferred_element_type=F32)


def _merge(x, a, b, gates, wpa, wpb, wo):
    n = x.shape[0]
    row = lambda w: pl.BlockSpec((ROW_TILE, w), lambda i: (i, 0))
    return pl.pallas_call(
        _merge_body,
        grid=(n // ROW_TILE,),
        in_specs=[row(D_MODEL), row(A_WIDTH), row(B_V), row(2 * D_MODEL),
                  _const_spec((A_WIDTH, D_MODEL)), _const_spec((B_V, D_MODEL)),
                  _const_spec((D_MODEL, D_MODEL))],
        out_specs=row(D_MODEL),
        out_shape=jax.ShapeDtypeStruct((n, D_MODEL), F32),
        compiler_params=_params(1),
        name="merge",
    )(x, a, b, gates, wpa, wpb, wo)


def _layer(x, lp, tiles, lam, out_scale, rel_bias):
    b, s, _ = x.shape
    x0 = x.reshape(b * s, D_MODEL)
    x1 = _ffn(x0, lp["ffn1_norm"], lp["ffn1_w_in"], lp["ffn1_w_out"])
    a, q, k, v, gates = _proj(x1, lp["mix_norm"], lp["w_in"], lp["gate_bias"], lp["sgu_norm"],
                              lp["wcat"], lp["sbias"], lp["seg"], lp["qg"], lp["kg"])
    bo = _attn(rel_bias, lam, q.reshape(b, s, B_QK), k.reshape(b, s, B_QK),
               v.reshape(b, s, B_V), tiles, lp["diff_subln"], out_scale)
    x2 = _merge(x1, a, bo.reshape(b * s, B_V), gates, lp["w_proj_a"], lp["w_proj_b"], lp["w_out"])
    y = _ffn(x2, lp["ffn2_norm"], lp["ffn2_w_in"], lp["ffn2_w_out"], lp["final_norm"])
    return y.reshape(b, s, D_MODEL)


def kernel(x_prompt, x_sample, rel_bias, ffn1_norm, ffn1_w_in, ffn1_w_out, mix_norm, w_in, gate_bias, sgu_norm, sgu_w, sgu_b, q_norm, k_norm, lambda_q1, lambda_k1, lambda_q2, lambda_k2, diff_subln, w_proj_a, w_proj_b, w_out, ffn2_norm, ffn2_w_in, ffn2_w_out, final_norm):
    depth = ffn1_norm.shape[0]
    y_prompt, y_sample = x_prompt, x_sample
    row = lambda p: p.reshape(1, -1).astype(F32)
    for l in range(depth):
        lambda_init = 0.8 - 0.6 * math.exp(-0.3 * l)
        sw = sgu_w[l].astype(BF16)
        lp = {
            "ffn1_norm": row(ffn1_norm[l]), "ffn1_w_in": ffn1_w_in[l].astype(BF16),
            "ffn1_w_out": ffn1_w_out[l].astype(BF16),
            "mix_norm": row(mix_norm[l]), "w_in": w_in[l].astype(BF16),
            "gate_bias": row(gate_bias[l]), "sgu_norm": row(sgu_norm[l]),
            "wcat": jnp.concatenate([sw[0::2], sw[1::2]], axis=-1),
            "sbias": jnp.repeat(sgu_b[l].T.astype(F32), A_GROUP_DIM, axis=1),
            "seg": jnp.kron(jnp.eye(B_QK // HEAD_DIM, dtype=F32),
                            jnp.full((HEAD_DIM, HEAD_DIM), 1.0 / HEAD_DIM, F32)).astype(BF16),
            "qg": jnp.tile(row(q_norm[l]) * HEAD_DIM ** -0.5, (1, B_QK // HEAD_DIM)),
            "kg": jnp.tile(row(k_norm[l]), (1, B_QK // HEAD_DIM)),
            "diff_subln": row(diff_subln[l]),
            "w_proj_a": w_proj_a[l].astype(BF16), "w_proj_b": w_proj_b[l].astype(BF16),
            "w_out": w_out[l].astype(BF16),
            "ffn2_norm": row(ffn2_norm[l]), "ffn2_w_in": ffn2_w_in[l].astype(BF16),
            "ffn2_w_out": ffn2_w_out[l].astype(BF16), "final_norm": row(final_norm[l]),
        }
        tiles, lam = _prep(rel_bias.astype(F32), row(lambda_q1[l]), row(lambda_k1[l]),
                           row(lambda_q2[l]), row(lambda_k2[l]), lambda_init)
        out_scale = 1.0 - lambda_init
        y_prompt = _layer(y_prompt, lp, tiles, lam, out_scale, rel_bias.astype(F32))
        y_sample = _layer(y_sample, lp, tiles, lam, out_scale, rel_bias.astype(F32))
    return (y_prompt, y_sample)
```

```python
import functools
import math

import jax
import jax.numpy as jnp
from jax import lax
from jax.experimental import pallas as pl
from jax.experimental.pallas import tpu as pltpu

D_MODEL = 1024
D_FF = 2816
CHUNK = 128
A_WIDTH = 512
A_GROUPS = 8
A_GROUP_DIM = A_WIDTH // A_GROUPS
B_HEADS = 4
HEAD_DIM = 64
HEAD_WIDTH = 2 * HEAD_DIM
B_QK = B_HEADS * HEAD_WIDTH
B_V = B_HEADS * HEAD_WIDTH
N_BUCKETS = 32
MAX_DISTANCE = 128
IN_COLS = 2 * A_WIDTH + 2 * B_QK + B_V + 2 * D_MODEL
EPS = 1e-6

V7X_LANES = 128
V7X_VMEM_LIMIT_BYTES = 56 * 1024 * 1024

ROW_TILE = 256
ATTN_TILE = 256
N_BIAS_TILES = 5

F32 = jnp.float32
BF16 = jnp.bfloat16


def _rms(x, g):
    return x * lax.rsqrt(jnp.mean(x * x, axis=-1, keepdims=True) + EPS) * g


def _const_spec(shape):
    return pl.BlockSpec(shape, lambda *_: (0,) * len(shape), pipeline_mode=pl.Buffered(1))


def _params(n_axes):
    return pltpu.CompilerParams(dimension_semantics=("parallel",) * n_axes,
                                vmem_limit_bytes=V7X_VMEM_LIMIT_BYTES)


def _ffn_body(x_ref, g_ref, win_ref, wout_ref, *rest, final_norm):
    o_ref = rest[-1]
    x = x_ref[...]
    h = _rms(x, g_ref[...]).astype(BF16)
    gu = jnp.dot(h, win_ref[...], preferred_element_type=F32)
    gate = gu[:, :D_FF]
    up = gu[:, D_FF:]
    act = (gate * jax.nn.sigmoid(gate) * up).astype(BF16)
    y = x + 0.5 * jnp.dot(act, wout_ref[...], preferred_element_type=F32)
    if final_norm:
        y = _rms(y, rest[0][...])
    o_ref[...] = y


def _ffn(x, g, w_in, w_out, final_g=None):
    n = x.shape[0]
    row = pl.BlockSpec((ROW_TILE, D_MODEL), lambda i: (i, 0))
    in_specs = [row, _const_spec((1, D_MODEL)), _const_spec((D_MODEL, 2 * D_FF)),
                _const_spec((D_FF, D_MODEL))]
    args = [x, g, w_in, w_out]
    if final_g is not None:
        in_specs.append(_const_spec((1, D_MODEL)))
        args.append(final_g)
    return pl.pallas_call(
        functools.partial(_ffn_body, final_norm=final_g is not None),
        grid=(n // ROW_TILE,),
        in_specs=in_specs,
        out_specs=row,
        out_shape=jax.ShapeDtypeStruct((n, D_MODEL), F32),
        compiler_params=_params(1),
        name="ffn_final" if final_g is not None else "ffn",
    )(*args)


def _proj_body(x_ref, g_ref, win_ref, gbias_ref, sgug_ref, wcat_ref, sbias_ref, seg_ref,
               qg_ref, kg_ref, a_ref, q_ref, k_ref, v_ref, gate_ref):
    tm = x_ref.shape[0]
    h = _rms(x_ref[...], g_ref[...]).astype(BF16)
    proj = jnp.dot(h, win_ref[...], preferred_element_type=F32)
    c0 = 2 * A_WIDTH
    c1 = c0 + B_QK
    c2 = c1 + B_QK
    c3 = c2 + B_V

    uv = jax.nn.gelu(proj[:, :c0])
    u = uv[:, :A_WIDTH]
    vn = _rms(uv[:, A_WIDTH:], sgug_ref[...])
    lane = lax.broadcasted_iota(jnp.int32, (tm, A_WIDTH), 1)
    low = (lane & (V7X_LANES - 1)) < A_GROUP_DIM
    v_low = jnp.where(low, vn, 0.0).astype(BF16)
    v_high = jnp.where(low, 0.0, vn).astype(BF16)
    for c in range(tm // CHUNK):
        rows = slice(c * CHUNK, (c + 1) * CHUNK)
        slabs = []
        for p in range(A_WIDTH // V7X_LANES):
            cols = slice(p * V7X_LANES, (p + 1) * V7X_LANES)
            rhs = jnp.concatenate([v_low[rows, cols], v_high[rows, cols]], axis=0)
            slabs.append(jnp.dot(wcat_ref[p], rhs, preferred_element_type=F32))
        s = jnp.concatenate(slabs, axis=1) + sbias_ref[...]
        a_ref[rows, :] = (u[rows, :] * s).astype(a_ref.dtype)

    def seg_norm(t, gain):
        ms = jnp.dot((t * t).astype(BF16), seg_ref[...], preferred_element_type=F32)
        return (t * lax.rsqrt(ms + EPS) * gain).astype(BF16)

    q_ref[...] = seg_norm(proj[:, c0:c1], qg_ref[...])
    k_ref[...] = seg_norm(proj[:, c1:c2], kg_ref[...])
    v_ref[...] = proj[:, c2:c3].astype(BF16)
    gate_ref[...] = jax.nn.sigmoid(proj[:, c3:] + gbias_ref[...]).astype(gate_ref.dtype)


def _proj(x, g, w_in, gate_bias, sgu_g, wcat, sbias, seg, qg, kg):
    n = x.shape[0]
    row = lambda w: pl.BlockSpec((ROW_TILE, w), lambda i: (i, 0))
    out_w = (A_WIDTH, B_QK, B_QK, B_V, 2 * D_MODEL)
    return pl.pallas_call(
        _proj_body,
        grid=(n // ROW_TILE,),
        in_specs=[row(D_MODEL), _const_spec((1, D_MODEL)), _const_spec((D_MODEL, IN_COLS)),
                  _const_spec((1, 2 * D_MODEL)), _const_spec((1, A_WIDTH)),
                  _const_spec(wcat.shape), _const_spec(sbias.shape), _const_spec(seg.shape),
                  _const_spec((1, B_QK)), _const_spec((1, B_QK))],
        out_specs=[row(w) for w in out_w],
        out_shape=[jax.ShapeDtypeStruct((n, w), BF16) for w in out_w],
        compiler_params=_params(1),
        name="proj",
    )(x, g, w_in, gate_bias, sgu_g, wcat, sbias, seg, qg, kg)


def _bucket_thresholds():
    half = N_BUCKETS // 2
    max_exact = half // 2
    n_log = half - max_exact
    ratio = MAX_DISTANCE // max_exact
    out = []
    for j in range(1, n_log):
        target = max_exact ** n_log * ratio ** j
        n = max_exact
        while n ** n_log < target:
            n += 1
        out.append(n)
    return max_exact, out


def _prep_body(rb_ref, lq1_ref, lk1_ref, lq2_ref, lk2_ref, tiles_ref, lam_ref, *, lambda_init):
    t = ATTN_TILE
    half = N_BUCKETS // 2
    max_exact, thresholds = _bucket_thresholds()
    row = lax.broadcasted_iota(jnp.int32, (t, t), 0)
    col = lax.broadcasted_iota(jnp.int32, (t, t), 1)
    for hd in range(B_HEADS):
        tiles_ref[hd, 0] = jnp.full((t, t), rb_ref[half - 1, hd], F32)
        tiles_ref[hd, N_BIAS_TILES - 1] = jnp.full((t, t), rb_ref[N_BUCKETS - 1, hd], F32)
    for d in (-1, 0, 1):
        rel = col - row + d * t
        n = jnp.abs(rel)
        large = jnp.full((t, t), max_exact, jnp.int32)
        for th in thresholds:
            large = large + jnp.where(n >= th, 1, 0)
        bucket = jnp.where(rel > 0, half, 0) + jnp.where(n < max_exact, n, large)
        for hd in range(B_HEADS):
            tile = jnp.zeros((t, t), F32)
            for b in range(N_BUCKETS):
                tile = jnp.where(bucket == b, rb_ref[b, hd], tile)
            tiles_ref[hd, d + N_BIAS_TILES // 2] = tile
    s1 = jnp.sum(lq1_ref[...] * lk1_ref[...], axis=-1, keepdims=True)
    s2 = jnp.sum(lq2_ref[...] * lk2_ref[...], axis=-1, keepdims=True)
    lam = jnp.exp(s1) - jnp.exp(s2) + lambda_init
    lam_ref[...] = jnp.broadcast_to(lam, lam_ref.shape)


def _prep(rel_bias, lq1, lk1, lq2, lk2, lambda_init):
    t = ATTN_TILE
    vec = pl.BlockSpec((1, HEAD_DIM), lambda: (0, 0))
    return pl.pallas_call(
        functools.partial(_prep_body, lambda_init=lambda_init),
        in_specs=[pl.BlockSpec(memory_space=pltpu.SMEM), vec, vec, vec, vec],
        out_specs=[pl.BlockSpec((B_HEADS, N_BIAS_TILES, t, t), lambda: (0, 0, 0, 0)),
                   pl.BlockSpec((8, V7X_LANES), lambda: (0, 0))],
        out_shape=[jax.ShapeDtypeStruct((B_HEADS, N_BIAS_TILES, t, t), F32),
                   jax.ShapeDtypeStruct((8, V7X_LANES), F32)],
        name="prep",
    )(rel_bias, lq1, lk1, lq2, lk2)


def _attn_body(lam_ref, q_ref, k_ref, v_ref, tiles_ref, subg_ref, o_ref,
               s_ref, m_ref, mnext_ref, l_ref, acc_ref, *, out_scale):
    t = ATTN_TILE
    nk = k_ref.shape[0] // t
    nq = q_ref.shape[0] // t
    lane = lax.broadcasted_iota(jnp.int32, (t, HEAD_WIDTH), 1)
    first = jnp.where(lane < HEAD_DIM, 1.0, 0.0).astype(BF16)
    reps = t // V7X_LANES

    def q_stack(c):
        q = q_ref[pl.ds(pl.multiple_of(c * t, t), t), :]
        return jnp.concatenate([q * first, q * (1.0 - first)], axis=0)

    def logits_step(c, q2, j, buf):
        cols = slice(j * t, (j + 1) * t)
        bias = tiles_ref[jnp.clip(j - c, -2, 2) + N_BIAS_TILES // 2]
        s = lax.dot_general(q2, k_ref[cols, :], (((1,), (1,)), ((), ())),
                            preferred_element_type=F32)
        s = s + jnp.concatenate([bias, bias], axis=0)
        s_ref[buf, :, cols] = s
        blockmax = s[:, :V7X_LANES]
        for r in range(1, reps):
            blockmax = jnp.maximum(blockmax, s[:, r * V7X_LANES:(r + 1) * V7X_LANES])
        if j == 0:
            mnext_ref[...] = blockmax
        else:
            mnext_ref[...] = jnp.maximum(mnext_ref[...], blockmax)

    def finish_logits():
        m = jnp.max(mnext_ref[...], axis=1, keepdims=True)
        m_ref[...] = jnp.broadcast_to(m, m_ref.shape)

    def prob_step(j, buf):
        cols = slice(j * t, (j + 1) * t)
        p = jnp.exp(s_ref[buf, :, cols] - jnp.concatenate([m_ref[...]] * reps, axis=1))
        psum = p[:, :V7X_LANES]
        for r in range(1, reps):
            psum = psum + p[:, r * V7X_LANES:(r + 1) * V7X_LANES]
        pv = jnp.dot(p.astype(BF16), v_ref[cols, :], preferred_element_type=F32)
        if j == 0:
            l_ref[...] = psum
            acc_ref[...] = pv
        else:
            l_ref[...] += psum
            acc_ref[...] += pv

    def finish_block(c):
        l = jnp.sum(l_ref[...], axis=1, keepdims=True)
        res = acc_ref[...] / l
        out = res[:t] - lam_ref[0:1, 0:1] * res[t:]
        o_ref[pl.ds(pl.multiple_of(c * t, t), t), :] = (
            _rms(out, subg_ref[...]) * out_scale).astype(o_ref.dtype)

    def overlapped(c, buf):
        q2n = q_stack(c + 1)
        for j in range(nk):
            logits_step(c + 1, q2n, j, 1 - buf)
            prob_step(j, buf)
        finish_block(c)
        finish_logits()

    q2 = q_stack(0)
    for j in range(nk):
        logits_step(0, q2, j, 0)
    finish_logits()

    def body(h, carry):
        overlapped(2 * h, 0)
        overlapped(2 * h + 1, 1)
        return carry

    lax.fori_loop(0, nq // 2 - 1, body, 0)
    overlapped(nq - 2, 0)

    for j in range(nk):
        prob_step(j, 1)
    finish_block(nq - 1)


def _attn(lam, q, k, v, tiles, subg, out_scale):
    b, s, _ = q.shape
    t = ATTN_TILE
    seq = pl.BlockSpec((None, s, HEAD_WIDTH), lambda bi, hi: (bi, 0, hi))
    stack = pltpu.VMEM((2 * t, V7X_LANES), F32)
    return pl.pallas_call(
        functools.partial(_attn_body, out_scale=out_scale),
        grid=(b, B_HEADS),
        in_specs=[pl.BlockSpec((8, V7X_LANES), lambda bi, hi: (0, 0)),
                  seq, seq, seq,
                  pl.BlockSpec((None, N_BIAS_TILES, t, t), lambda bi, hi: (hi, 0, 0, 0)),
                  pl.BlockSpec((1, HEAD_WIDTH), lambda bi, hi: (0, 0))],
        out_specs=seq,
        out_shape=jax.ShapeDtypeStruct((b, s, B_V), BF16),
        scratch_shapes=[pltpu.VMEM((2, 2 * t, s), F32), stack, stack, stack, stack],
        compiler_params=_params(2),
        name="attn",
    )(lam, q, k, v, tiles, subg)


def _merge_body(x_ref, a_ref, b_ref, gate_ref, wpa_ref, wpb_ref, wo_ref, o_ref):
    ga = gate_ref[:, :D_MODEL].astype(F32)
    gb = gate_ref[:, D_MODEL:].astype(F32)
    merged = (ga * jnp.dot(a_ref[...], wpa_ref[...], preferred_element_type=F32)
              + gb * jnp.dot(b_ref[...], wpb_ref[...], preferred_element_type=F32))
    o_ref[...] = x_ref[...] + jnp.dot(merged.astype(BF16), wo_ref[...],
                                      preferred_element_type=F32)


def _merge(x, a, b, gates, wpa, wpb, wo):
    n = x.shape[0]
    row = lambda w: pl.BlockSpec((ROW_TILE, w), lambda i: (i, 0))
    return pl.pallas_call(
        _merge_body,
        grid=(n // ROW_TILE,),
        in_specs=[row(D_MODEL), row(A_WIDTH), row(B_V), row(2 * D_MODEL),
                  _const_spec((A_WIDTH, D_MODEL)), _const_spec((B_V, D_MODEL)),
                  _const_spec((D_MODEL, D_MODEL))],
        out_specs=row(D_MODEL),
        out_shape=jax.ShapeDtypeStruct((n, D_MODEL), F32),
        compiler_params=_params(1),
        name="merge",
    )(x, a, b, gates, wpa, wpb, wo)


def _layer(x, lp, tiles, lam, out_scale):
    b, s, _ = x.shape
    x0 = x.reshape(b * s, D_MODEL)
    x1 = _ffn(x0, lp["ffn1_norm"], lp["ffn1_w_in"], lp["ffn1_w_out"])
    a, q, k, v, gates = _proj(x1, lp["mix_norm"], lp["w_in"], lp["gate_bias"], lp["sgu_norm"],
                              lp["wcat"], lp["sbias"], lp["seg"], lp["qg"], lp["kg"])
    bo = _attn(lam, q.reshape(b, s, B_QK), k.reshape(b, s, B_QK),
               v.reshape(b, s, B_V), tiles, lp["diff_subln"], out_scale)
    x2 = _merge(x1, a, bo.reshape(b * s, B_V), gates, lp["w_proj_a"], lp["w_proj_b"], lp["w_out"])
    y = _ffn(x2, lp["ffn2_norm"], lp["ffn2_w_in"], lp["ffn2_w_out"], lp["final_norm"])
    return y.reshape(b, s, D_MODEL)


def kernel(x_prompt, x_sample, rel_bias, ffn1_norm, ffn1_w_in, ffn1_w_out, mix_norm, w_in, gate_bias, sgu_norm, sgu_w, sgu_b, q_norm, k_norm, lambda_q1, lambda_k1, lambda_q2, lambda_k2, diff_subln, w_proj_a, w_proj_b, w_out, ffn2_norm, ffn2_w_in, ffn2_w_out, final_norm):
    depth = ffn1_norm.shape[0]
    y_prompt, y_sample = x_prompt, x_sample
    row = lambda p: p.reshape(1, -1).astype(F32)
    for l in range(depth):
        lambda_init = 0.8 - 0.6 * math.exp(-0.3 * l)
        sw = sgu_w[l].astype(BF16)
        lp = {
            "ffn1_norm": row(ffn1_norm[l]), "ffn1_w_in": ffn1_w_in[l].astype(BF16),
            "ffn1_w_out": ffn1_w_out[l].astype(BF16),
            "mix_norm": row(mix_norm[l]), "w_in": w_in[l].astype(BF16),
            "gate_bias": row(gate_bias[l]), "sgu_norm": row(sgu_norm[l]),
            "wcat": jnp.concatenate([sw[0::2], sw[1::2]], axis=-1),
            "sbias": jnp.repeat(sgu_b[l].T.astype(F32), A_GROUP_DIM, axis=1),
            "seg": jnp.kron(jnp.eye(B_QK // HEAD_DIM, dtype=F32),
                            jnp.full((HEAD_DIM, HEAD_DIM), 1.0 / HEAD_DIM, F32)).astype(BF16),
            "qg": jnp.tile(row(q_norm[l]) * HEAD_DIM ** -0.5, (1, B_QK // HEAD_DIM)),
            "kg": jnp.tile(row(k_norm[l]), (1, B_QK // HEAD_DIM)),
            "diff_subln": row(diff_subln[l]),
            "w_proj_a": w_proj_a[l].astype(BF16), "w_proj_b": w_proj_b[l].astype(BF16),
            "w_out": w_out[l].astype(BF16),
            "ffn2_norm": row(ffn2_norm[l]), "ffn2_w_in": ffn2_w_in[l].astype(BF16),
            "ffn2_w_out": ffn2_w_out[l].astype(BF16), "final_norm": row(final_norm[l]),
        }
        tiles, lam = _prep(rel_bias.astype(F32), row(lambda_q1[l]), row(lambda_k1[l]),
                           row(lambda_q2[l]), row(lambda_k2[l]), lambda_init)
        out_scale = 1.0 - lambda_init
        y_prompt = _layer(y_prompt, lp, tiles, lam, out_scale)
        y_sample = _layer(y_sample, lp, tiles, lam, out_scale)
    return (y_prompt, y_sample)
```

```python
import functools
import math

import jax
import jax.numpy as jnp
from jax import lax
from jax.experimental import pallas as pl
from jax.experimental.pallas import tpu as pltpu

D_MODEL = 1024
D_FF = 2816
CHUNK = 128
A_WIDTH = 512
A_GROUPS = 8
A_GROUP_DIM = A_WIDTH // A_GROUPS
B_HEADS = 4
HEAD_DIM = 64
HEAD_WIDTH = 2 * HEAD_DIM
B_QK = B_HEADS * HEAD_WIDTH
B_V = B_HEADS * HEAD_WIDTH
N_BUCKETS = 32
MAX_DISTANCE = 128
IN_COLS = 2 * A_WIDTH + 2 * B_QK + B_V + 2 * D_MODEL
EPS = 1e-6

V7X_LANES = 128
V7X_VMEM_LIMIT_BYTES = 56 * 1024 * 1024

ROW_TILE = 256
ATTN_TILE = 256
ATTN_KEY_STEP = 512
LOG2E = math.log2(math.e)
N_BIAS_TILES = 5

F32 = jnp.float32
BF16 = jnp.bfloat16


def _rms(x, g):
    return x * lax.rsqrt(jnp.mean(x * x, axis=-1, keepdims=True) + EPS) * g


def _const_spec(shape):
    return pl.BlockSpec(shape, lambda *_: (0,) * len(shape), pipeline_mode=pl.Buffered(1))


def _params(n_axes):
    return pltpu.CompilerParams(dimension_semantics=("parallel",) * n_axes,
                                vmem_limit_bytes=V7X_VMEM_LIMIT_BYTES)


def _ffn_body(x_ref, g_ref, win_ref, wout_ref, *rest, final_norm):
    o_ref = rest[-1]
    x = x_ref[...]
    h = _rms(x, g_ref[...]).astype(BF16)
    gu = jnp.dot(h, win_ref[...], preferred_element_type=F32)
    gate = gu[:, :D_FF]
    up = gu[:, D_FF:]
    act = (gate * jax.nn.sigmoid(gate) * up).astype(BF16)
    y = x + 0.5 * jnp.dot(act, wout_ref[...], preferred_element_type=F32)
    if final_norm:
        y = _rms(y, rest[0][...])
    o_ref[...] = y


def _ffn(x, g, w_in, w_out, final_g=None):
    n = x.shape[0]
    row = pl.BlockSpec((ROW_TILE, D_MODEL), lambda i: (i, 0))
    in_specs = [row, _const_spec((1, D_MODEL)), _const_spec((D_MODEL, 2 * D_FF)),
                _const_spec((D_FF, D_MODEL))]
    args = [x, g, w_in, w_out]
    if final_g is not None:
        in_specs.append(_const_spec((1, D_MODEL)))
        args.append(final_g)
    return pl.pallas_call(
        functools.partial(_ffn_body, final_norm=final_g is not None),
        grid=(n // ROW_TILE,),
        in_specs=in_specs,
        out_specs=row,
        out_shape=jax.ShapeDtypeStruct((n, D_MODEL), F32),
        compiler_params=_params(1),
        name="ffn_final" if final_g is not None else "ffn",
    )(*args)


def _proj_body(x_ref, g_ref, win_ref, gbias_ref, sgug_ref, wcat_ref, sbias_ref, seg_ref,
               qg_ref, kg_ref, a_ref, q_ref, k_ref, v_ref, gate_ref):
    tm = x_ref.shape[0]
    h = _rms(x_ref[...], g_ref[...]).astype(BF16)
    proj = jnp.dot(h, win_ref[...], preferred_element_type=F32)
    c0 = 2 * A_WIDTH
    c1 = c0 + B_QK
    c2 = c1 + B_QK
    c3 = c2 + B_V

    uv = jax.nn.gelu(proj[:, :c0])
    u = uv[:, :A_WIDTH]
    vn = _rms(uv[:, A_WIDTH:], sgug_ref[...])
    lane = lax.broadcasted_iota(jnp.int32, (tm, A_WIDTH), 1)
    low = (lane & (V7X_LANES - 1)) < A_GROUP_DIM
    v_low = jnp.where(low, vn, 0.0).astype(BF16)
    v_high = jnp.where(low, 0.0, vn).astype(BF16)
    for c in range(tm // CHUNK):
        rows = slice(c * CHUNK, (c + 1) * CHUNK)
        slabs = []
        for p in range(A_WIDTH // V7X_LANES):
            cols = slice(p * V7X_LANES, (p + 1) * V7X_LANES)
            rhs = jnp.concatenate([v_low[rows, cols], v_high[rows, cols]], axis=0)
            slabs.append(jnp.dot(wcat_ref[p], rhs, preferred_element_type=F32))
        s = jnp.concatenate(slabs, axis=1) + sbias_ref[...]
        a_ref[rows, :] = (u[rows, :] * s).astype(a_ref.dtype)

    def seg_norm(t, gain):
        ms = jnp.dot((t * t).astype(BF16), seg_ref[...], preferred_element_type=F32)
        return (t * lax.rsqrt(ms + EPS) * gain).astype(BF16)

    q_ref[...] = seg_norm(proj[:, c0:c1], qg_ref[...])
    k_ref[...] = seg_norm(proj[:, c1:c2], kg_ref[...])
    v_ref[...] = proj[:, c2:c3].astype(BF16)
    gate_ref[...] = jax.nn.sigmoid(proj[:, c3:] + gbias_ref[...]).astype(gate_ref.dtype)


def _proj(x, g, w_in, gate_bias, sgu_g, wcat, sbias, seg, qg, kg):
    n = x.shape[0]
    row = lambda w: pl.BlockSpec((ROW_TILE, w), lambda i: (i, 0))
    out_w = (A_WIDTH, B_QK, B_QK, B_V, 2 * D_MODEL)
    return pl.pallas_call(
        _proj_body,
        grid=(n // ROW_TILE,),
        in_specs=[row(D_MODEL), _const_spec((1, D_MODEL)), _const_spec((D_MODEL, IN_COLS)),
                  _const_spec((1, 2 * D_MODEL)), _const_spec((1, A_WIDTH)),
                  _const_spec(wcat.shape), _const_spec(sbias.shape), _const_spec(seg.shape),
                  _const_spec((1, B_QK)), _const_spec((1, B_QK))],
        out_specs=[row(w) for w in out_w],
        out_shape=[jax.ShapeDtypeStruct((n, w), BF16) for w in out_w],
        compiler_params=_params(1),
        name="proj",
    )(x, g, w_in, gate_bias, sgu_g, wcat, sbias, seg, qg, kg)


def _bucket_thresholds():
    half = N_BUCKETS // 2
    max_exact = half // 2
    n_log = half - max_exact
    ratio = MAX_DISTANCE // max_exact
    out = []
    for j in range(1, n_log):
        target = max_exact ** n_log * ratio ** j
        n = max_exact
        while n ** n_log < target:
            n += 1
        out.append(n)
    return max_exact, out


def _prep_body(rb_ref, lq1_ref, lk1_ref, lq2_ref, lk2_ref, tiles_ref, lam_ref, *, lambda_init):
    t = ATTN_TILE
    half = N_BUCKETS // 2
    max_exact, thresholds = _bucket_thresholds()
    row = lax.broadcasted_iota(jnp.int32, (t, t), 0)
    col = lax.broadcasted_iota(jnp.int32, (t, t), 1)
    for hd in range(B_HEADS):
        tiles_ref[hd, 0] = jnp.full((t, t), rb_ref[half - 1, hd] * LOG2E, F32)
        tiles_ref[hd, N_BIAS_TILES - 1] = jnp.full((t, t), rb_ref[N_BUCKETS - 1, hd] * LOG2E, F32)
    for d in (-1, 0, 1):
        rel = col - row + d * t
        n = jnp.abs(rel)
        large = jnp.full((t, t), max_exact, jnp.int32)
        for th in thresholds:
            large = large + jnp.where(n >= th, 1, 0)
        bucket = jnp.where(rel > 0, half, 0) + jnp.where(n < max_exact, n, large)
        for hd in range(B_HEADS):
            tile = jnp.zeros((t, t), F32)
            for b in range(N_BUCKETS):
                tile = jnp.where(bucket == b, rb_ref[b, hd] * LOG2E, tile)
            tiles_ref[hd, d + N_BIAS_TILES // 2] = tile
    s1 = jnp.sum(lq1_ref[...] * lk1_ref[...], axis=-1, keepdims=True)
    s2 = jnp.sum(lq2_ref[...] * lk2_ref[...], axis=-1, keepdims=True)
    lam = jnp.exp(s1) - jnp.exp(s2) + lambda_init
    lam_ref[...] = jnp.broadcast_to(lam, lam_ref.shape)


def _prep(rel_bias, lq1, lk1, lq2, lk2, lambda_init):
    t = ATTN_TILE
    vec = pl.BlockSpec((1, HEAD_DIM), lambda: (0, 0))
    return pl.pallas_call(
        functools.partial(_prep_body, lambda_init=lambda_init),
        in_specs=[pl.BlockSpec(memory_space=pltpu.SMEM), vec, vec, vec, vec],
        out_specs=[pl.BlockSpec((B_HEADS, N_BIAS_TILES, t, t), lambda: (0, 0, 0, 0)),
                   pl.BlockSpec((8, V7X_LANES), lambda: (0, 0))],
        out_shape=[jax.ShapeDtypeStruct((B_HEADS, N_BIAS_TILES, t, t), F32),
                   jax.ShapeDtypeStruct((8, V7X_LANES), F32)],
        name="prep",
    )(rel_bias, lq1, lk1, lq2, lk2)


def _attn_body(lam_ref, q_ref, k_ref, v_ref, tiles_ref, subg_ref, o_ref,
               s_ref, m_ref, mnext_ref, l_ref, acc_ref, *, out_scale):
    t = ATTN_TILE
    ks = ATTN_KEY_STEP
    nk = k_ref.shape[0] // ks
    nq = q_ref.shape[0] // t
    lane = lax.broadcasted_iota(jnp.int32, (t, HEAD_WIDTH), 1)
    first = jnp.where(lane < HEAD_DIM, 1.0, 0.0).astype(BF16)
    reps = ks // V7X_LANES

    def q_stack(c):
        q = q_ref[pl.ds(pl.multiple_of(c * t, t), t), :]
        return jnp.concatenate([q * first, q * (1.0 - first)], axis=0)

    def logits_step(c, q2, j):
        cols = slice(j * ks, (j + 1) * ks)
        bias = jnp.concatenate(
            [tiles_ref[jnp.clip(jt - c, -2, 2) + N_BIAS_TILES // 2]
             for jt in range(j * ks // t, (j + 1) * ks // t)], axis=1)
        s = lax.dot_general(q2, k_ref[cols, :], (((1,), (1,)), ((), ())),
                            preferred_element_type=F32)
        s = s + jnp.concatenate([bias, bias], axis=0)
        s_ref[:, cols] = s
        blockmax = s[:, :V7X_LANES]
        for r in range(1, reps):
            blockmax = jnp.maximum(blockmax, s[:, r * V7X_LANES:(r + 1) * V7X_LANES])
        if j == 0:
            mnext_ref[...] = blockmax
        else:
            mnext_ref[...] = jnp.maximum(mnext_ref[...], blockmax)

    def finish_logits():
        m = jnp.max(mnext_ref[...], axis=1, keepdims=True)
        m_ref[...] = jnp.broadcast_to(m, m_ref.shape)

    def prob_step(j, s):
        cols = slice(j * ks, (j + 1) * ks)
        p = jnp.exp2(s - jnp.concatenate([m_ref[...]] * reps, axis=1))
        psum = p[:, :V7X_LANES]
        for r in range(1, reps):
            psum = psum + p[:, r * V7X_LANES:(r + 1) * V7X_LANES]
        pv = jnp.dot(p.astype(BF16), v_ref[cols, :], preferred_element_type=F32)
        if j == 0:
            l_ref[...] = psum
            acc_ref[...] = pv
        else:
            l_ref[...] += psum
            acc_ref[...] += pv

    def finish_block(c):
        l = jnp.sum(l_ref[...], axis=1, keepdims=True)
        res = acc_ref[...] / l
        out = res[:t] - lam_ref[0:1, 0:1] * res[t:]
        o_ref[pl.ds(pl.multiple_of(c * t, t), t), :] = (
            _rms(out, subg_ref[...]) * out_scale).astype(o_ref.dtype)

    q2 = q_stack(0)
    for j in range(nk):
        logits_step(0, q2, j)
    finish_logits()

    def body(c, carry):
        q2n = q_stack(c + 1)
        for j in range(nk):
            s = s_ref[:, j * ks:(j + 1) * ks]
            logits_step(c + 1, q2n, j)
            prob_step(j, s)
        finish_block(c)
        finish_logits()
        return carry

    lax.fori_loop(0, nq - 1, body, 0)

    for j in range(nk):
        prob_step(j, s_ref[:, j * ks:(j + 1) * ks])
    finish_block(nq - 1)


def _attn(lam, q, k, v, tiles, subg, out_scale):
    b, s, _ = q.shape
    t = ATTN_TILE
    seq = pl.BlockSpec((None, s, HEAD_WIDTH), lambda bi, hi: (bi, 0, hi))
    stack = pltpu.VMEM((2 * t, V7X_LANES), F32)
    return pl.pallas_call(
        functools.partial(_attn_body, out_scale=out_scale),
        grid=(b, B_HEADS),
        in_specs=[pl.BlockSpec((8, V7X_LANES), lambda bi, hi: (0, 0)),
                  seq, seq, seq,
                  pl.BlockSpec((None, N_BIAS_TILES, t, t), lambda bi, hi: (hi, 0, 0, 0)),
                  pl.BlockSpec((1, HEAD_WIDTH), lambda bi, hi: (0, 0))],
        out_specs=seq,
        out_shape=jax.ShapeDtypeStruct((b, s, B_V), BF16),
        scratch_shapes=[pltpu.VMEM((2 * t, s), F32), stack, stack, stack, stack],
        compiler_params=_params(2),
        name="attn",
    )(lam, q, k, v, tiles, subg)


def _merge_body(x_ref, a_ref, b_ref, gate_ref, wpa_ref, wpb_ref, wo_ref, o_ref):
    ga = gate_ref[:, :D_MODEL].astype(F32)
    gb = gate_ref[:, D_MODEL:].astype(F32)
    merged = (ga * jnp.dot(a_ref[...], wpa_ref[...], preferred_element_type=F32)
              + gb * jnp.dot(b_ref[...], wpb_ref[...], preferred_element_type=F32))
    o_ref[...] = x_ref[...] + jnp.dot(merged.astype(BF16), wo_ref[...],
                                      preferred_element_type=F32)


def _merge(x, a, b, gates, wpa, wpb, wo):
    n = x.shape[0]
    row = lambda w: pl.BlockSpec((ROW_TILE, w), lambda i: (i, 0))
    return pl.pallas_call(
        _merge_body,
        grid=(n // ROW_TILE,),
        in_specs=[row(D_MODEL), row(A_WIDTH), row(B_V), row(2 * D_MODEL),
                  _const_spec((A_WIDTH, D_MODEL)), _const_spec((B_V, D_MODEL)),
                  _const_spec((D_MODEL, D_MODEL))],
        out_specs=row(D_MODEL),
        out_shape=jax.ShapeDtypeStruct((n, D_MODEL), F32),
        compiler_params=_params(1),
        name="merge",
    )(x, a, b, gates, wpa, wpb, wo)


def _layer(x, lp, tiles, lam, out_scale):
    b, s, _ = x.shape
    x0 = x.reshape(b * s, D_MODEL)
    x1 = _ffn(x0, lp["ffn1_norm"], lp["ffn1_w_in"], lp["ffn1_w_out"])
    a, q, k, v, gates = _proj(x1, lp["mix_norm"], lp["w_in"], lp["gate_bias"], lp["sgu_norm"],
                              lp["wcat"], lp["sbias"], lp["seg"], lp["qg"], lp["kg"])
    bo = _attn(lam, q.reshape(b, s, B_QK), k.reshape(b, s, B_QK),
               v.reshape(b, s, B_V), tiles, lp["diff_subln"], out_scale)
    x2 = _merge(x1, a, bo.reshape(b * s, B_V), gates, lp["w_proj_a"], lp["w_proj_b"], lp["w_out"])
    y = _ffn(x2, lp["ffn2_norm"], lp["ffn2_w_in"], lp["ffn2_w_out"], lp["final_norm"])
    return y.reshape(b, s, D_MODEL)


def kernel(x_prompt, x_sample, rel_bias, ffn1_norm, ffn1_w_in, ffn1_w_out, mix_norm, w_in, gate_bias, sgu_norm, sgu_w, sgu_b, q_norm, k_norm, lambda_q1, lambda_k1, lambda_q2, lambda_k2, diff_subln, w_proj_a, w_proj_b, w_out, ffn2_norm, ffn2_w_in, ffn2_w_out, final_norm):
    depth = ffn1_norm.shape[0]
    y_prompt, y_sample = x_prompt, x_sample
    row = lambda p: p.reshape(1, -1).astype(F32)
    for l in range(depth):
        lambda_init = 0.8 - 0.6 * math.exp(-0.3 * l)
        sw = sgu_w[l].astype(BF16)
        lp = {
            "ffn1_norm": row(ffn1_norm[l]), "ffn1_w_in": ffn1_w_in[l].astype(BF16),
            "ffn1_w_out": ffn1_w_out[l].astype(BF16),
            "mix_norm": row(mix_norm[l]), "w_in": w_in[l].astype(BF16),
            "gate_bias": row(gate_bias[l]), "sgu_norm": row(sgu_norm[l]),
            "wcat": jnp.concatenate([sw[0::2], sw[1::2]], axis=-1),
            "sbias": jnp.repeat(sgu_b[l].T.astype(F32), A_GROUP_DIM, axis=1),
            "seg": jnp.kron(jnp.eye(B_QK // HEAD_DIM, dtype=F32),
                            jnp.full((HEAD_DIM, HEAD_DIM), 1.0 / HEAD_DIM, F32)).astype(BF16),
            "qg": jnp.tile(row(q_norm[l]) * (HEAD_DIM ** -0.5 * LOG2E), (1, B_QK // HEAD_DIM)),
            "kg": jnp.tile(row(k_norm[l]), (1, B_QK // HEAD_DIM)),
            "diff_subln": row(diff_subln[l]),
            "w_proj_a": w_proj_a[l].astype(BF16), "w_proj_b": w_proj_b[l].astype(BF16),
            "w_out": w_out[l].astype(BF16),
            "ffn2_norm": row(ffn2_norm[l]), "ffn2_w_in": ffn2_w_in[l].astype(BF16),
            "ffn2_w_out": ffn2_w_out[l].astype(BF16), "final_norm": row(final_norm[l]),
        }
        tiles, lam = _prep(rel_bias.astype(F32), row(lambda_q1[l]), row(lambda_k1[l]),
                           row(lambda_q2[l]), row(lambda_k2[l]), lambda_init)
        out_scale = 1.0 - lambda_init
        y_prompt = _layer(y_prompt, lp, tiles, lam, out_scale)
        y_sample = _layer(y_sample, lp, tiles, lam, out_scale)
    return (y_prompt, y_sample)
```

```python
import functools
import math

import jax
import jax.numpy as jnp
from jax import lax
from jax.experimental import pallas as pl
from jax.experimental.pallas import tpu as pltpu

D_MODEL = 1024
D_FF = 2816
CHUNK = 128
A_WIDTH = 512
A_GROUPS = 8
A_GROUP_DIM = A_WIDTH // A_GROUPS
B_HEADS = 4
HEAD_DIM = 64
HEAD_WIDTH = 2 * HEAD_DIM
B_QK = B_HEADS * HEAD_WIDTH
B_V = B_HEADS * HEAD_WIDTH
N_BUCKETS = 32
MAX_DISTANCE = 128
IN_COLS = 2 * A_WIDTH + 2 * B_QK + B_V + 2 * D_MODEL
EPS = 1e-6

V7X_LANES = 128
V7X_VMEM_LIMIT_BYTES = 56 * 1024 * 1024

ROW_TILE = 512
ATTN_TILE = 256
ATTN_KEY_STEP = 512
LOG2E = math.log2(math.e)
N_BIAS_TILES = 5

F32 = jnp.float32
BF16 = jnp.bfloat16


def _rms(x, g):
    return x * lax.rsqrt(jnp.mean(x * x, axis=-1, keepdims=True) + EPS) * g


def _const_spec(shape):
    return pl.BlockSpec(shape, lambda *_: (0,) * len(shape), pipeline_mode=pl.Buffered(1))


def _params(n_axes):
    return pltpu.CompilerParams(dimension_semantics=("parallel",) * n_axes,
                                vmem_limit_bytes=V7X_VMEM_LIMIT_BYTES)


def _ffn_body(x_ref, g_ref, win_ref, wout_ref, *rest, final_norm):
    o_ref = rest[-1]
    x = x_ref[...]
    h = _rms(x, g_ref[...]).astype(BF16)
    gu = jnp.dot(h, win_ref[...], preferred_element_type=F32)
    gate = gu[:, :D_FF]
    up = gu[:, D_FF:]
    act = (gate * jax.nn.sigmoid(gate) * up).astype(BF16)
    y = x + 0.5 * jnp.dot(act, wout_ref[...], preferred_element_type=F32)
    if final_norm:
        y = _rms(y, rest[0][...])
    o_ref[...] = y


def _ffn(x, g, w_in, w_out, final_g=None):
    n = x.shape[0]
    row = pl.BlockSpec((ROW_TILE, D_MODEL), lambda i: (i, 0))
    in_specs = [row, _const_spec((1, D_MODEL)), _const_spec((D_MODEL, 2 * D_FF)),
                _const_spec((D_FF, D_MODEL))]
    args = [x, g, w_in, w_out]
    if final_g is not None:
        in_specs.append(_const_spec((1, D_MODEL)))
        args.append(final_g)
    return pl.pallas_call(
        functools.partial(_ffn_body, final_norm=final_g is not None),
        grid=(n // ROW_TILE,),
        in_specs=in_specs,
        out_specs=row,
        out_shape=jax.ShapeDtypeStruct((n, D_MODEL), F32),
        compiler_params=_params(1),
        name="ffn_final" if final_g is not None else "ffn",
    )(*args)


def _proj_body(x_ref, g_ref, win_ref, gbias_ref, sgug_ref, wcat_ref, sbias_ref, seg_ref,
               qg_ref, kg_ref, a_ref, q_ref, k_ref, v_ref, gate_ref):
    tm = x_ref.shape[0]
    h = _rms(x_ref[...], g_ref[...]).astype(BF16)
    proj = jnp.dot(h, win_ref[...], preferred_element_type=F32)
    c0 = 2 * A_WIDTH
    c1 = c0 + B_QK
    c2 = c1 + B_QK
    c3 = c2 + B_V

    uv = jax.nn.gelu(proj[:, :c0])
    u = uv[:, :A_WIDTH]
    vn = _rms(uv[:, A_WIDTH:], sgug_ref[...])
    lane = lax.broadcasted_iota(jnp.int32, (tm, A_WIDTH), 1)
    low = (lane & (V7X_LANES - 1)) < A_GROUP_DIM
    v_low = jnp.where(low, vn, 0.0).astype(BF16)
    v_high = jnp.where(low, 0.0, vn).astype(BF16)
    for c in range(tm // CHUNK):
        rows = slice(c * CHUNK, (c + 1) * CHUNK)
        slabs = []
        for p in range(A_WIDTH // V7X_LANES):
            cols = slice(p * V7X_LANES, (p + 1) * V7X_LANES)
            rhs = jnp.concatenate([v_low[rows, cols], v_high[rows, cols]], axis=0)
            slabs.append(jnp.dot(wcat_ref[p], rhs, preferred_element_type=F32))
        s = jnp.concatenate(slabs, axis=1) + sbias_ref[...]
        a_ref[rows, :] = (u[rows, :] * s).astype(a_ref.dtype)

    def seg_norm(t, gain):
        ms = jnp.dot((t * t).astype(BF16), seg_ref[...], preferred_element_type=F32)
        return (t * lax.rsqrt(ms + EPS) * gain).astype(BF16)

    q_ref[...] = seg_norm(proj[:, c0:c1], qg_ref[...])
    k_ref[...] = seg_norm(proj[:, c1:c2], kg_ref[...])
    v_ref[...] = proj[:, c2:c3].astype(BF16)
    gate_ref[...] = jax.nn.sigmoid(proj[:, c3:] + gbias_ref[...]).astype(gate_ref.dtype)


def _proj(x, g, w_in, gate_bias, sgu_g, wcat, sbias, seg, qg, kg):
    n = x.shape[0]
    row = lambda w: pl.BlockSpec((ROW_TILE, w), lambda i: (i, 0))
    out_w = (A_WIDTH, B_QK, B_QK, B_V, 2 * D_MODEL)
    return pl.pallas_call(
        _proj_body,
        grid=(n // ROW_TILE,),
        in_specs=[row(D_MODEL), _const_spec((1, D_MODEL)), _const_spec((D_MODEL, IN_COLS)),
                  _const_spec((1, 2 * D_MODEL)), _const_spec((1, A_WIDTH)),
                  _const_spec(wcat.shape), _const_spec(sbias.shape), _const_spec(seg.shape),
                  _const_spec((1, B_QK)), _const_spec((1, B_QK))],
        out_specs=[row(w) for w in out_w],
        out_shape=[jax.ShapeDtypeStruct((n, w), BF16) for w in out_w],
        compiler_params=_params(1),
        name="proj",
    )(x, g, w_in, gate_bias, sgu_g, wcat, sbias, seg, qg, kg)


def _bucket_thresholds():
    half = N_BUCKETS // 2
    max_exact = half // 2
    n_log = half - max_exact
    ratio = MAX_DISTANCE // max_exact
    out = []
    for j in range(1, n_log):
        target = max_exact ** n_log * ratio ** j
        n = max_exact
        while n ** n_log < target:
            n += 1
        out.append(n)
    return max_exact, out


def _prep_body(rb_ref, lq1_ref, lk1_ref, lq2_ref, lk2_ref, tiles_ref, lam_ref, *, lambda_init):
    t = ATTN_TILE
    half = N_BUCKETS // 2
    max_exact, thresholds = _bucket_thresholds()
    row = lax.broadcasted_iota(jnp.int32, (t, t), 0)
    col = lax.broadcasted_iota(jnp.int32, (t, t), 1)
    for hd in range(B_HEADS):
        tiles_ref[hd, 0] = jnp.full((t, t), rb_ref[half - 1, hd] * LOG2E, F32)
        tiles_ref[hd, N_BIAS_TILES - 1] = jnp.full((t, t), rb_ref[N_BUCKETS - 1, hd] * LOG2E, F32)
    for d in (-1, 0, 1):
        rel = col - row + d * t
        n = jnp.abs(rel)
        large = jnp.full((t, t), max_exact, jnp.int32)
        for th in thresholds:
            large = large + jnp.where(n >= th, 1, 0)
        bucket = jnp.where(rel > 0, half, 0) + jnp.where(n < max_exact, n, large)
        for hd in range(B_HEADS):
            tile = jnp.zeros((t, t), F32)
            for b in range(N_BUCKETS):
                tile = jnp.where(bucket == b, rb_ref[b, hd] * LOG2E, tile)
            tiles_ref[hd, d + N_BIAS_TILES // 2] = tile
    s1 = jnp.sum(lq1_ref[...] * lk1_ref[...], axis=-1, keepdims=True)
    s2 = jnp.sum(lq2_ref[...] * lk2_ref[...], axis=-1, keepdims=True)
    lam = jnp.exp(s1) - jnp.exp(s2) + lambda_init
    lam_ref[...] = jnp.broadcast_to(lam, lam_ref.shape)


def _prep(rel_bias, lq1, lk1, lq2, lk2, lambda_init):
    t = ATTN_TILE
    vec = pl.BlockSpec((1, HEAD_DIM), lambda: (0, 0))
    return pl.pallas_call(
        functools.partial(_prep_body, lambda_init=lambda_init),
        in_specs=[pl.BlockSpec(memory_space=pltpu.SMEM), vec, vec, vec, vec],
        out_specs=[pl.BlockSpec((B_HEADS, N_BIAS_TILES, t, t), lambda: (0, 0, 0, 0)),
                   pl.BlockSpec((8, V7X_LANES), lambda: (0, 0))],
        out_shape=[jax.ShapeDtypeStruct((B_HEADS, N_BIAS_TILES, t, t), F32),
                   jax.ShapeDtypeStruct((8, V7X_LANES), F32)],
        name="prep",
    )(rel_bias, lq1, lk1, lq2, lk2)


def _attn_body(lam_ref, q_ref, k_ref, v_ref, tiles_ref, subg_ref, o_ref,
               s_ref, m_ref, mnext_ref, l_ref, acc_ref, *, out_scale):
    t = ATTN_TILE
    ks = ATTN_KEY_STEP
    nk = k_ref.shape[0] // ks
    nq = q_ref.shape[0] // t
    lane = lax.broadcasted_iota(jnp.int32, (t, HEAD_WIDTH), 1)
    first = jnp.where(lane < HEAD_DIM, 1.0, 0.0).astype(BF16)
    reps = ks // V7X_LANES

    def q_stack(c):
        q = q_ref[pl.ds(pl.multiple_of(c * t, t), t), :]
        return jnp.concatenate([q * first, q * (1.0 - first)], axis=0)

    def logits_step(c, q2, j):
        cols = slice(j * ks, (j + 1) * ks)
        bias = jnp.concatenate(
            [tiles_ref[jnp.clip(jt - c, -2, 2) + N_BIAS_TILES // 2]
             for jt in range(j * ks // t, (j + 1) * ks // t)], axis=1)
        s = lax.dot_general(q2, k_ref[cols, :], (((1,), (1,)), ((), ())),
                            preferred_element_type=F32)
        s = s + jnp.concatenate([bias, bias], axis=0)
        s_ref[:, cols] = s
        blockmax = s[:, :V7X_LANES]
        for r in range(1, reps):
            blockmax = jnp.maximum(blockmax, s[:, r * V7X_LANES:(r + 1) * V7X_LANES])
        if j == 0:
            mnext_ref[...] = blockmax
        else:
            mnext_ref[...] = jnp.maximum(mnext_ref[...], blockmax)

    def finish_logits():
        m = jnp.max(mnext_ref[...], axis=1, keepdims=True)
        m_ref[...] = jnp.broadcast_to(m, m_ref.shape)

    def prob_step(j, s):
        cols = slice(j * ks, (j + 1) * ks)
        p = jnp.exp2(s - jnp.concatenate([m_ref[...]] * reps, axis=1))
        psum = p[:, :V7X_LANES]
        for r in range(1, reps):
            psum = psum + p[:, r * V7X_LANES:(r + 1) * V7X_LANES]
        pv = jnp.dot(p.astype(BF16), v_ref[cols, :], preferred_element_type=F32)
        if j == 0:
            l_ref[...] = psum
            acc_ref[...] = pv
        else:
            l_ref[...] += psum
            acc_ref[...] += pv

    def finish_block(c):
        l = jnp.sum(l_ref[...], axis=1, keepdims=True)
        res = acc_ref[...] / l
        out = res[:t] - lam_ref[0:1, 0:1] * res[t:]
        o_ref[pl.ds(pl.multiple_of(c * t, t), t), :] = (
            _rms(out, subg_ref[...]) * out_scale).astype(o_ref.dtype)

    q2 = q_stack(0)
    for j in range(nk):
        logits_step(0, q2, j)
    finish_logits()

    def body(c, carry):
        q2n = q_stack(c + 1)
        for j in range(nk):
            s = s_ref[:, j * ks:(j + 1) * ks]
            logits_step(c + 1, q2n, j)
            prob_step(j, s)
        finish_block(c)
        finish_logits()
        return carry

    lax.fori_loop(0, nq - 1, body, 0)

    for j in range(nk):
        prob_step(j, s_ref[:, j * ks:(j + 1) * ks])
    finish_block(nq - 1)


def _attn(lam, q, k, v, tiles, subg, out_scale):
    b, s, _ = q.shape
    t = ATTN_TILE
    seq = pl.BlockSpec((None, s, HEAD_WIDTH), lambda bi, hi: (bi, 0, hi))
    stack = pltpu.VMEM((2 * t, V7X_LANES), F32)
    return pl.pallas_call(
        functools.partial(_attn_body, out_scale=out_scale),
        grid=(b, B_HEADS),
        in_specs=[pl.BlockSpec((8, V7X_LANES), lambda bi, hi: (0, 0)),
                  seq, seq, seq,
                  pl.BlockSpec((None, N_BIAS_TILES, t, t), lambda bi, hi: (hi, 0, 0, 0)),
                  pl.BlockSpec((1, HEAD_WIDTH), lambda bi, hi: (0, 0))],
        out_specs=seq,
        out_shape=jax.ShapeDtypeStruct((b, s, B_V), BF16),
        scratch_shapes=[pltpu.VMEM((2 * t, s), F32), stack, stack, stack, stack],
        compiler_params=_params(2),
        name="attn",
    )(lam, q, k, v, tiles, subg)


def _merge_body(x_ref, a_ref, b_ref, gate_ref, wpa_ref, wpb_ref, wo_ref, o_ref):
    ga = gate_ref[:, :D_MODEL].astype(F32)
    gb = gate_ref[:, D_MODEL:].astype(F32)
    merged = (ga * jnp.dot(a_ref[...], wpa_ref[...], preferred_element_type=F32)
              + gb * jnp.dot(b_ref[...], wpb_ref[...], preferred_element_type=F32))
    o_ref[...] = x_ref[...] + jnp.dot(merged.astype(BF16), wo_ref[...],
                                      preferred_element_type=F32)


def _merge(x, a, b, gates, wpa, wpb, wo):
    n = x.shape[0]
    row = lambda w: pl.BlockSpec((ROW_TILE, w), lambda i: (i, 0))
    return pl.pallas_call(
        _merge_body,
        grid=(n // ROW_TILE,),
        in_specs=[row(D_MODEL), row(A_WIDTH), row(B_V), row(2 * D_MODEL),
                  _const_spec((A_WIDTH, D_MODEL)), _const_spec((B_V, D_MODEL)),
                  _const_spec((D_MODEL, D_MODEL))],
        out_specs=row(D_MODEL),
        out_shape=jax.ShapeDtypeStruct((n, D_MODEL), F32),
        compiler_params=_params(1),
        name="merge",
    )(x, a, b, gates, wpa, wpb, wo)


def _layer(x, lp, tiles, lam, out_scale):
    b, s, _ = x.shape
    x0 = x.reshape(b * s, D_MODEL)
    x1 = _ffn(x0, lp["ffn1_norm"], lp["ffn1_w_in"], lp["ffn1_w_out"])
    a, q, k, v, gates = _proj(x1, lp["mix_norm"], lp["w_in"], lp["gate_bias"], lp["sgu_norm"],
                              lp["wcat"], lp["sbias"], lp["seg"], lp["qg"], lp["kg"])
    bo = _attn(lam, q.reshape(b, s, B_QK), k.reshape(b, s, B_QK),
               v.reshape(b, s, B_V), tiles, lp["diff_subln"], out_scale)
    x2 = _merge(x1, a, bo.reshape(b * s, B_V), gates, lp["w_proj_a"], lp["w_proj_b"], lp["w_out"])
    y = _ffn(x2, lp["ffn2_norm"], lp["ffn2_w_in"], lp["ffn2_w_out"], lp["final_norm"])
    return y.reshape(b, s, D_MODEL)


def kernel(x_prompt, x_sample, rel_bias, ffn1_norm, ffn1_w_in, ffn1_w_out, mix_norm, w_in, gate_bias, sgu_norm, sgu_w, sgu_b, q_norm, k_norm, lambda_q1, lambda_k1, lambda_q2, lambda_k2, diff_subln, w_proj_a, w_proj_b, w_out, ffn2_norm, ffn2_w_in, ffn2_w_out, final_norm):
    depth = ffn1_norm.shape[0]
    y_prompt, y_sample = x_prompt, x_sample
    row = lambda p: p.reshape(1, -1).astype(F32)
    for l in range(depth):
        lambda_init = 0.8 - 0.6 * math.exp(-0.3 * l)
        sw = sgu_w[l].astype(BF16)
        lp = {
            "ffn1_norm": row(ffn1_norm[l]), "ffn1_w_in": ffn1_w_in[l].astype(BF16),
            "ffn1_w_out": ffn1_w_out[l].astype(BF16),
            "mix_norm": row(mix_norm[l]), "w_in": w_in[l].astype(BF16),
            "gate_bias": row(gate_bias[l]), "sgu_norm": row(sgu_norm[l]),
            "wcat": jnp.concatenate([sw[0::2], sw[1::2]], axis=-1),
            "sbias": jnp.repeat(sgu_b[l].T.astype(F32), A_GROUP_DIM, axis=1),
            "seg": jnp.kron(jnp.eye(B_QK // HEAD_DIM, dtype=F32),
                            jnp.full((HEAD_DIM, HEAD_DIM), 1.0 / HEAD_DIM, F32)).astype(BF16),
            "qg": jnp.tile(row(q_norm[l]) * (HEAD_DIM ** -0.5 * LOG2E), (1, B_QK // HEAD_DIM)),
            "kg": jnp.tile(row(k_norm[l]), (1, B_QK // HEAD_DIM)),
            "diff_subln": row(diff_subln[l]),
            "w_proj_a": w_proj_a[l].astype(BF16), "w_proj_b": w_proj_b[l].astype(BF16),
            "w_out": w_out[l].astype(BF16),
            "ffn2_norm": row(ffn2_norm[l]), "ffn2_w_in": ffn2_w_in[l].astype(BF16),
            "ffn2_w_out": ffn2_w_out[l].astype(BF16), "final_norm": row(final_norm[l]),
        }
        tiles, lam = _prep(rel_bias.astype(F32), row(lambda_q1[l]), row(lambda_k1[l]),
                           row(lambda_q2[l]), row(lambda_k2[l]), lambda_init)
        out_scale = 1.0 - lambda_init
        y_prompt = _layer(y_prompt, lp, tiles, lam, out_scale)
        y_sample = _layer(y_sample, lp, tiles, lam, out_scale)
    return (y_prompt, y_sample)
```

```python
import functools
import math

import jax
import jax.numpy as jnp
from jax import lax
from jax.experimental import pallas as pl
from jax.experimental.pallas import tpu as pltpu

D_MODEL = 1024
D_FF = 2816
CHUNK = 128
A_WIDTH = 512
A_GROUPS = 8
A_GROUP_DIM = A_WIDTH // A_GROUPS
B_HEADS = 4
HEAD_DIM = 64
HEAD_WIDTH = 2 * HEAD_DIM
B_QK = B_HEADS * HEAD_WIDTH
B_V = B_HEADS * HEAD_WIDTH
N_BUCKETS = 32
MAX_DISTANCE = 128
IN_COLS = 2 * A_WIDTH + 2 * B_QK + B_V + 2 * D_MODEL
EPS = 1e-6

V7X_LANES = 128
V7X_VMEM_LIMIT_BYTES = 56 * 1024 * 1024

ROW_TILE = 512
ATTN_TILE = 256
ATTN_KEY_STEP = 512
LOG2E = math.log2(math.e)
LOGIT_SPAN = 100.0
N_BIAS_TILES = 5

F32 = jnp.float32
BF16 = jnp.bfloat16


def _rms(x, g):
    return x * lax.rsqrt(jnp.mean(x * x, axis=-1, keepdims=True) + EPS) * g


def _const_spec(shape):
    return pl.BlockSpec(shape, lambda *_: (0,) * len(shape), pipeline_mode=pl.Buffered(1))


def _params(n_axes):
    return pltpu.CompilerParams(dimension_semantics=("parallel",) * n_axes,
                                vmem_limit_bytes=V7X_VMEM_LIMIT_BYTES)


def _ffn_body(x_ref, g_ref, win_ref, wout_ref, *rest, final_norm):
    o_ref = rest[-1]
    x = x_ref[...]
    h = _rms(x, g_ref[...]).astype(BF16)
    gu = jnp.dot(h, win_ref[...], preferred_element_type=F32)
    gate = gu[:, :D_FF]
    up = gu[:, D_FF:]
    act = (gate * jax.nn.sigmoid(gate) * up).astype(BF16)
    y = x + 0.5 * jnp.dot(act, wout_ref[...], preferred_element_type=F32)
    if final_norm:
        y = _rms(y, rest[0][...])
    o_ref[...] = y


def _ffn(x, g, w_in, w_out, final_g=None):
    n = x.shape[0]
    row = pl.BlockSpec((ROW_TILE, D_MODEL), lambda i: (i, 0))
    in_specs = [row, _const_spec((1, D_MODEL)), _const_spec((D_MODEL, 2 * D_FF)),
                _const_spec((D_FF, D_MODEL))]
    args = [x, g, w_in, w_out]
    if final_g is not None:
        in_specs.append(_const_spec((1, D_MODEL)))
        args.append(final_g)
    return pl.pallas_call(
        functools.partial(_ffn_body, final_norm=final_g is not None),
        grid=(n // ROW_TILE,),
        in_specs=in_specs,
        out_specs=row,
        out_shape=jax.ShapeDtypeStruct((n, D_MODEL), F32),
        compiler_params=_params(1),
        name="ffn_final" if final_g is not None else "ffn",
    )(*args)


def _proj_body(x_ref, g_ref, win_ref, gbias_ref, sgug_ref, wcat_ref, sbias_ref, seg_ref,
               qg_ref, kg_ref, a_ref, q_ref, k_ref, v_ref, gate_ref):
    tm = x_ref.shape[0]
    h = _rms(x_ref[...], g_ref[...]).astype(BF16)
    proj = jnp.dot(h, win_ref[...], preferred_element_type=F32)
    c0 = 2 * A_WIDTH
    c1 = c0 + B_QK
    c2 = c1 + B_QK
    c3 = c2 + B_V

    uv = jax.nn.gelu(proj[:, :c0])
    u = uv[:, :A_WIDTH]
    vn = _rms(uv[:, A_WIDTH:], sgug_ref[...])
    lane = lax.broadcasted_iota(jnp.int32, (tm, A_WIDTH), 1)
    low = (lane & (V7X_LANES - 1)) < A_GROUP_DIM
    v_low = jnp.where(low, vn, 0.0).astype(BF16)
    v_high = jnp.where(low, 0.0, vn).astype(BF16)
    for c in range(tm // CHUNK):
        rows = slice(c * CHUNK, (c + 1) * CHUNK)
        slabs = []
        for p in range(A_WIDTH // V7X_LANES):
            cols = slice(p * V7X_LANES, (p + 1) * V7X_LANES)
            rhs = jnp.concatenate([v_low[rows, cols], v_high[rows, cols]], axis=0)
            slabs.append(jnp.dot(wcat_ref[p], rhs, preferred_element_type=F32))
        s = jnp.concatenate(slabs, axis=1) + sbias_ref[...]
        a_ref[rows, :] = (u[rows, :] * s).astype(a_ref.dtype)

    def seg_norm(t, gain):
        ms = jnp.dot((t * t).astype(BF16), seg_ref[...], preferred_element_type=F32)
        return (t * lax.rsqrt(ms + EPS) * gain).astype(BF16)

    q_ref[...] = seg_norm(proj[:, c0:c1], qg_ref[...])
    k_ref[...] = seg_norm(proj[:, c1:c2], kg_ref[...])
    v_ref[...] = proj[:, c2:c3].astype(BF16)
    gate_ref[...] = jax.nn.sigmoid(proj[:, c3:] + gbias_ref[...]).astype(gate_ref.dtype)


def _proj(x, g, w_in, gate_bias, sgu_g, wcat, sbias, seg, qg, kg):
    n = x.shape[0]
    row = lambda w: pl.BlockSpec((ROW_TILE, w), lambda i: (i, 0))
    out_w = (A_WIDTH, B_QK, B_QK, B_V, 2 * D_MODEL)
    return pl.pallas_call(
        _proj_body,
        grid=(n // ROW_TILE,),
        in_specs=[row(D_MODEL), _const_spec((1, D_MODEL)), _const_spec((D_MODEL, IN_COLS)),
                  _const_spec((1, 2 * D_MODEL)), _const_spec((1, A_WIDTH)),
                  _const_spec(wcat.shape), _const_spec(sbias.shape), _const_spec(seg.shape),
                  _const_spec((1, B_QK)), _const_spec((1, B_QK))],
        out_specs=[row(w) for w in out_w],
        out_shape=[jax.ShapeDtypeStruct((n, w), BF16) for w in out_w],
        compiler_params=_params(1),
        name="proj",
    )(x, g, w_in, gate_bias, sgu_g, wcat, sbias, seg, qg, kg)


def _bucket_thresholds():
    half = N_BUCKETS // 2
    max_exact = half // 2
    n_log = half - max_exact
    ratio = MAX_DISTANCE // max_exact
    out = []
    for j in range(1, n_log):
        target = max_exact ** n_log * ratio ** j
        n = max_exact
        while n ** n_log < target:
            n += 1
        out.append(n)
    return max_exact, out


def _prep_body(rb_ref, shift_ref, lq1_ref, lk1_ref, lq2_ref, lk2_ref, tiles_ref, lam_ref, *,
               lambda_init):
    t = ATTN_TILE
    half = N_BUCKETS // 2
    max_exact, thresholds = _bucket_thresholds()
    row = lax.broadcasted_iota(jnp.int32, (t, t), 0)
    col = lax.broadcasted_iota(jnp.int32, (t, t), 1)
    shift = shift_ref[0, 0]
    for hd in range(B_HEADS):
        tiles_ref[hd, 0] = jnp.full((t, t), rb_ref[half - 1, hd] * LOG2E - shift, F32)
        tiles_ref[hd, N_BIAS_TILES - 1] = jnp.full(
            (t, t), rb_ref[N_BUCKETS - 1, hd] * LOG2E - shift, F32)
    for d in (-1, 0, 1):
        rel = col - row + d * t
        n = jnp.abs(rel)
        large = jnp.full((t, t), max_exact, jnp.int32)
        for th in thresholds:
            large = large + jnp.where(n >= th, 1, 0)
        bucket = jnp.where(rel > 0, half, 0) + jnp.where(n < max_exact, n, large)
        for hd in range(B_HEADS):
            tile = jnp.zeros((t, t), F32)
            for b in range(N_BUCKETS):
                tile = jnp.where(bucket == b, rb_ref[b, hd] * LOG2E - shift, tile)
            tiles_ref[hd, d + N_BIAS_TILES // 2] = tile
    s1 = jnp.sum(lq1_ref[...] * lk1_ref[...], axis=-1, keepdims=True)
    s2 = jnp.sum(lq2_ref[...] * lk2_ref[...], axis=-1, keepdims=True)
    lam = jnp.exp(s1) - jnp.exp(s2) + lambda_init
    lam_ref[...] = jnp.broadcast_to(lam, lam_ref.shape)


def _prep(rel_bias, shift, lq1, lk1, lq2, lk2, lambda_init):
    t = ATTN_TILE
    vec = pl.BlockSpec((1, HEAD_DIM), lambda: (0, 0))
    return pl.pallas_call(
        functools.partial(_prep_body, lambda_init=lambda_init),
        in_specs=[pl.BlockSpec(memory_space=pltpu.SMEM), pl.BlockSpec(memory_space=pltpu.SMEM),
                  vec, vec, vec, vec],
        out_specs=[pl.BlockSpec((B_HEADS, N_BIAS_TILES, t, t), lambda: (0, 0, 0, 0)),
                   pl.BlockSpec((8, V7X_LANES), lambda: (0, 0))],
        out_shape=[jax.ShapeDtypeStruct((B_HEADS, N_BIAS_TILES, t, t), F32),
                   jax.ShapeDtypeStruct((8, V7X_LANES), F32)],
        name="prep",
    )(rel_bias, shift, lq1, lk1, lq2, lk2)


def _q_stack(q_ref, c):
    t = ATTN_TILE
    lane = lax.broadcasted_iota(jnp.int32, (t, HEAD_WIDTH), 1)
    first = jnp.where(lane < HEAD_DIM, 1.0, 0.0).astype(BF16)
    q = q_ref[pl.ds(pl.multiple_of(c * t, t), t), :]
    return jnp.concatenate([q * first, q * (1.0 - first)], axis=0)


def _biased_logits(q2, c, j, k_ref, tiles_ref):
    t = ATTN_TILE
    ks = ATTN_KEY_STEP
    bias = jnp.concatenate(
        [tiles_ref[jnp.clip(jt - c, -2, 2) + N_BIAS_TILES // 2]
         for jt in range(j * ks // t, (j + 1) * ks // t)], axis=1)
    s = lax.dot_general(q2, k_ref[j * ks:(j + 1) * ks, :], (((1,), (1,)), ((), ())),
                        preferred_element_type=F32)
    return s + jnp.concatenate([bias, bias], axis=0)


def _store_output(res, c, lam_ref, subg_ref, o_ref, out_scale):
    t = ATTN_TILE
    out = res[:t] - lam_ref[0:1, 0:1] * res[t:]
    o_ref[pl.ds(pl.multiple_of(c * t, t), t), :] = (
        _rms(out, subg_ref[...]) * out_scale).astype(o_ref.dtype)


def _attn_exact_body(lam_ref, q_ref, k_ref, v_ref, tiles_ref, subg_ref, o_ref,
                     s_ref, m_ref, mnext_ref, l_ref, acc_ref, *, out_scale):
    t = ATTN_TILE
    ks = ATTN_KEY_STEP
    nk = k_ref.shape[0] // ks
    nq = q_ref.shape[0] // t
    reps = ks // V7X_LANES

    def logits_step(c, q2, j):
        s = _biased_logits(q2, c, j, k_ref, tiles_ref)
        s_ref[:, j * ks:(j + 1) * ks] = s
        blockmax = s[:, :V7X_LANES]
        for r in range(1, reps):
            blockmax = jnp.maximum(blockmax, s[:, r * V7X_LANES:(r + 1) * V7X_LANES])
        if j == 0:
            mnext_ref[...] = blockmax
        else:
            mnext_ref[...] = jnp.maximum(mnext_ref[...], blockmax)

    def finish_logits():
        m = jnp.max(mnext_ref[...], axis=1, keepdims=True)
        m_ref[...] = jnp.broadcast_to(m, m_ref.shape)

    def prob_step(j, s):
        p = jnp.exp2(s - jnp.concatenate([m_ref[...]] * reps, axis=1))
        psum = p[:, :V7X_LANES]
        for r in range(1, reps):
            psum = psum + p[:, r * V7X_LANES:(r + 1) * V7X_LANES]
        pv = jnp.dot(p.astype(BF16), v_ref[j * ks:(j + 1) * ks, :], preferred_element_type=F32)
        if j == 0:
            l_ref[...] = psum
            acc_ref[...] = pv
        else:
            l_ref[...] += psum
            acc_ref[...] += pv

    def finish_block(c):
        l = jnp.sum(l_ref[...], axis=1, keepdims=True)
        _store_output(acc_ref[...] / l, c, lam_ref, subg_ref, o_ref, out_scale)

    q2 = _q_stack(q_ref, 0)
    for j in range(nk):
        logits_step(0, q2, j)
    finish_logits()

    def body(c, carry):
        q2n = _q_stack(q_ref, c + 1)
        for j in range(nk):
            s = s_ref[:, j * ks:(j + 1) * ks]
            logits_step(c + 1, q2n, j)
            prob_step(j, s)
        finish_block(c)
        finish_logits()
        return carry

    lax.fori_loop(0, nq - 1, body, 0)

    for j in range(nk):
        prob_step(j, s_ref[:, j * ks:(j + 1) * ks])
    finish_block(nq - 1)


def _attn_fast_body(lam_ref, q_ref, k_ref, v_ref, tiles_ref, subg_ref, o_ref, p_ref, vaug_ref, *,
                    out_scale):
    t = ATTN_TILE
    ks = ATTN_KEY_STEP
    nk = k_ref.shape[0] // ks
    nq = q_ref.shape[0] // t

    vaug_ref[:, :HEAD_WIDTH] = v_ref[...]
    vaug_ref[:, HEAD_WIDTH:] = jnp.ones((v_ref.shape[0], HEAD_WIDTH), BF16)

    def prob_block(c):
        q2 = _q_stack(q_ref, c)
        for j in range(nk):
            s = _biased_logits(q2, c, j, k_ref, tiles_ref)
            p_ref[:, j * ks:(j + 1) * ks] = jnp.exp2(s).astype(BF16)

    def output_block(c):
        res = jnp.dot(p_ref[...], vaug_ref[...], preferred_element_type=F32)
        _store_output(res[:, :HEAD_WIDTH] / res[:, HEAD_WIDTH:], c, lam_ref, subg_ref, o_ref,
                      out_scale)

    prob_block(0)

    def body(c, carry):
        output_block(c - 1)
        prob_block(c)
        return carry

    lax.fori_loop(1, nq, body, 0, unroll=4)
    output_block(nq - 1)


def _attn(lam, q, k, v, tiles, subg, *, out_scale, single_pass):
    b, s, _ = q.shape
    t = ATTN_TILE
    seq = pl.BlockSpec((None, s, HEAD_WIDTH), lambda bi, hi: (bi, 0, hi))
    stack = pltpu.VMEM((2 * t, V7X_LANES), F32)
    if single_pass:
        body, name = _attn_fast_body, "attn_fast"
        scratch = [pltpu.VMEM((2 * t, s), BF16), pltpu.VMEM((s, 2 * HEAD_WIDTH), BF16)]
    else:
        body, name = _attn_exact_body, "attn_exact"
        scratch = [pltpu.VMEM((2 * t, s), F32), stack, stack, stack, stack]
    return pl.pallas_call(
        functools.partial(body, out_scale=out_scale),
        grid=(b, B_HEADS),
        in_specs=[pl.BlockSpec((8, V7X_LANES), lambda bi, hi: (0, 0)),
                  seq, seq, seq,
                  pl.BlockSpec((None, N_BIAS_TILES, t, t), lambda bi, hi: (hi, 0, 0, 0)),
                  pl.BlockSpec((1, HEAD_WIDTH), lambda bi, hi: (0, 0))],
        out_specs=seq,
        out_shape=jax.ShapeDtypeStruct((b, s, B_V), BF16),
        scratch_shapes=scratch,
        compiler_params=_params(2),
        name=name,
    )(lam, q, k, v, tiles, subg)


def _logit_shift(q_gain, k_gain, rel_bias):
    qk_bound = 1.02 * HEAD_DIM * jnp.max(jnp.abs(q_gain)) * jnp.max(jnp.abs(k_gain))
    bias = rel_bias.astype(F32) * LOG2E
    shift = qk_bound + jnp.max(bias)
    span = 2.0 * qk_bound + jnp.max(bias) - jnp.min(bias)
    return shift.reshape(1, 1), span <= LOGIT_SPAN


def _merge_body(x_ref, a_ref, b_ref, gate_ref, wpa_ref, wpb_ref, wo_ref, o_ref):
    ga = gate_ref[:, :D_MODEL].astype(F32)
    gb = gate_ref[:, D_MODEL:].astype(F32)
    merged = (ga * jnp.dot(a_ref[...], wpa_ref[...], preferred_element_type=F32)
              + gb * jnp.dot(b_ref[...], wpb_ref[...], preferred_element_type=F32))
    o_ref[...] = x_ref[...] + jnp.dot(merged.astype(BF16), wo_ref[...],
                                      preferred_element_type=F32)


def _merge(x, a, b, gates, wpa, wpb, wo):
    n = x.shape[0]
    row = lambda w: pl.BlockSpec((ROW_TILE, w), lambda i: (i, 0))
    return pl.pallas_call(
        _merge_body,
        grid=(n // ROW_TILE,),
        in_specs=[row(D_MODEL), row(A_WIDTH), row(B_V), row(2 * D_MODEL),
                  _const_spec((A_WIDTH, D_MODEL)), _const_spec((B_V, D_MODEL)),
                  _const_spec((D_MODEL, D_MODEL))],
        out_specs=row(D_MODEL),
        out_shape=jax.ShapeDtypeStruct((n, D_MODEL), F32),
        compiler_params=_params(1),
        name="merge",
    )(x, a, b, gates, wpa, wpb, wo)


def _layer(x, lp, tiles, lam, out_scale, single_pass):
    b, s, _ = x.shape
    x0 = x.reshape(b * s, D_MODEL)
    x1 = _ffn(x0, lp["ffn1_norm"], lp["ffn1_w_in"], lp["ffn1_w_out"])
    a, q, k, v, gates = _proj(x1, lp["mix_norm"], lp["w_in"], lp["gate_bias"], lp["sgu_norm"],
                              lp["wcat"], lp["sbias"], lp["seg"], lp["qg"], lp["kg"])
    bo = lax.cond(single_pass,
                  functools.partial(_attn, out_scale=out_scale, single_pass=True),
                  functools.partial(_attn, out_scale=out_scale, single_pass=False),
                  lam, q.reshape(b, s, B_QK), k.reshape(b, s, B_QK), v.reshape(b, s, B_V),
                  tiles, lp["diff_subln"])
    x2 = _merge(x1, a, bo.reshape(b * s, B_V), gates, lp["w_proj_a"], lp["w_proj_b"], lp["w_out"])
    y = _ffn(x2, lp["ffn2_norm"], lp["ffn2_w_in"], lp["ffn2_w_out"], lp["final_norm"])
    return y.reshape(b, s, D_MODEL)


def kernel(x_prompt, x_sample, rel_bias, ffn1_norm, ffn1_w_in, ffn1_w_out, mix_norm, w_in, gate_bias, sgu_norm, sgu_w, sgu_b, q_norm, k_norm, lambda_q1, lambda_k1, lambda_q2, lambda_k2, diff_subln, w_proj_a, w_proj_b, w_out, ffn2_norm, ffn2_w_in, ffn2_w_out, final_norm):
    depth = ffn1_norm.shape[0]
    y_prompt, y_sample = x_prompt, x_sample
    row = lambda p: p.reshape(1, -1).astype(F32)
    for l in range(depth):
        lambda_init = 0.8 - 0.6 * math.exp(-0.3 * l)
        sw = sgu_w[l].astype(BF16)
        lp = {
            "ffn1_norm": row(ffn1_norm[l]), "ffn1_w_in": ffn1_w_in[l].astype(BF16),
            "ffn1_w_out": ffn1_w_out[l].astype(BF16),
            "mix_norm": row(mix_norm[l]), "w_in": w_in[l].astype(BF16),
            "gate_bias": row(gate_bias[l]), "sgu_norm": row(sgu_norm[l]),
            "wcat": jnp.concatenate([sw[0::2], sw[1::2]], axis=-1),
            "sbias": jnp.repeat(sgu_b[l].T.astype(F32), A_GROUP_DIM, axis=1),
            "seg": jnp.kron(jnp.eye(B_QK // HEAD_DIM, dtype=F32),
                            jnp.full((HEAD_DIM, HEAD_DIM), 1.0 / HEAD_DIM, F32)).astype(BF16),
            "qg": jnp.tile(row(q_norm[l]) * (HEAD_DIM ** -0.5 * LOG2E), (1, B_QK // HEAD_DIM)),
            "kg": jnp.tile(row(k_norm[l]), (1, B_QK // HEAD_DIM)),
            "diff_subln": row(diff_subln[l]),
            "w_proj_a": w_proj_a[l].astype(BF16), "w_proj_b": w_proj_b[l].astype(BF16),
            "w_out": w_out[l].astype(BF16),
            "ffn2_norm": row(ffn2_norm[l]), "ffn2_w_in": ffn2_w_in[l].astype(BF16),
            "ffn2_w_out": ffn2_w_out[l].astype(BF16), "final_norm": row(final_norm[l]),
        }
        shift, single_pass = _logit_shift(lp["qg"], lp["kg"], rel_bias)
        tiles, lam = _prep(rel_bias.astype(F32), shift, row(lambda_q1[l]), row(lambda_k1[l]),
                           row(lambda_q2[l]), row(lambda_k2[l]), lambda_init)
        out_scale = 1.0 - lambda_init
        y_prompt = _layer(y_prompt, lp, tiles, lam, out_scale, single_pass)
        y_sample = _layer(y_sample, lp, tiles, lam, out_scale, single_pass)
    return (y_prompt, y_sample)
```

```python
import functools
import math

import jax
import jax.numpy as jnp
from jax import lax
from jax.experimental import pallas as pl
from jax.experimental.pallas import tpu as pltpu

D_MODEL = 1024
D_FF = 2816
CHUNK = 128
A_WIDTH = 512
A_GROUPS = 8
A_GROUP_DIM = A_WIDTH // A_GROUPS
B_HEADS = 4
HEAD_DIM = 64
HEAD_WIDTH = 2 * HEAD_DIM
B_QK = B_HEADS * HEAD_WIDTH
B_V = B_HEADS * HEAD_WIDTH
N_BUCKETS = 32
MAX_DISTANCE = 128
IN_COLS = 2 * A_WIDTH + 2 * B_QK + B_V + 2 * D_MODEL
EPS = 1e-6

V7X_LANES = 128
V7X_VMEM_LIMIT_BYTES = 56 * 1024 * 1024

ROW_TILE = 512
ATTN_TILE = 256
ATTN_KEY_STEP = 512
LOG2E = math.log2(math.e)
LOGIT_SPAN = 100.0
N_BIAS_TILES = 5
ONES_ROWS = 16

F32 = jnp.float32
BF16 = jnp.bfloat16


def _rms(x, g):
    return x * lax.rsqrt(jnp.mean(x * x, axis=-1, keepdims=True) + EPS) * g


def _const_spec(shape):
    return pl.BlockSpec(shape, lambda *_: (0,) * len(shape), pipeline_mode=pl.Buffered(1))


def _params(n_axes):
    return pltpu.CompilerParams(dimension_semantics=("parallel",) * n_axes,
                                vmem_limit_bytes=V7X_VMEM_LIMIT_BYTES)


def _ffn_body(x_ref, g_ref, win_ref, wout_ref, *rest, final_norm):
    o_ref = rest[-1]
    x = x_ref[...]
    h = _rms(x, g_ref[...]).astype(BF16)
    gu = jnp.dot(h, win_ref[...], preferred_element_type=F32)
    gate = gu[:, :D_FF]
    up = gu[:, D_FF:]
    act = (gate * jax.nn.sigmoid(gate) * up).astype(BF16)
    y = x + 0.5 * jnp.dot(act, wout_ref[...], preferred_element_type=F32)
    if final_norm:
        y = _rms(y, rest[0][...])
    o_ref[...] = y


def _ffn(x, g, w_in, w_out, final_g=None):
    n = x.shape[0]
    row = pl.BlockSpec((ROW_TILE, D_MODEL), lambda i: (i, 0))
    in_specs = [row, _const_spec((1, D_MODEL)), _const_spec((D_MODEL, 2 * D_FF)),
                _const_spec((D_FF, D_MODEL))]
    args = [x, g, w_in, w_out]
    if final_g is not None:
        in_specs.append(_const_spec((1, D_MODEL)))
        args.append(final_g)
    return pl.pallas_call(
        functools.partial(_ffn_body, final_norm=final_g is not None),
        grid=(n // ROW_TILE,),
        in_specs=in_specs,
        out_specs=row,
        out_shape=jax.ShapeDtypeStruct((n, D_MODEL), F32),
        compiler_params=_params(1),
        name="ffn_final" if final_g is not None else "ffn",
    )(*args)


def _proj_body(x_ref, g_ref, win_ref, gbias_ref, sgug_ref, wcat_ref, sbias_ref, seg_ref,
               qg_ref, kg_ref, a_ref, q_ref, k_ref, v_ref, gate_ref):
    tm = x_ref.shape[0]
    h = _rms(x_ref[...], g_ref[...]).astype(BF16)
    proj = jnp.dot(h, win_ref[...], preferred_element_type=F32)
    c0 = 2 * A_WIDTH
    c1 = c0 + B_QK
    c2 = c1 + B_QK
    c3 = c2 + B_V

    uv = jax.nn.gelu(proj[:, :c0])
    u = uv[:, :A_WIDTH]
    vn = _rms(uv[:, A_WIDTH:], sgug_ref[...])
    lane = lax.broadcasted_iota(jnp.int32, (tm, A_WIDTH), 1)
    low = (lane & (V7X_LANES - 1)) < A_GROUP_DIM
    v_low = jnp.where(low, vn, 0.0).astype(BF16)
    v_high = jnp.where(low, 0.0, vn).astype(BF16)
    for c in range(tm // CHUNK):
        rows = slice(c * CHUNK, (c + 1) * CHUNK)
        slabs = []
        for p in range(A_WIDTH // V7X_LANES):
            cols = slice(p * V7X_LANES, (p + 1) * V7X_LANES)
            rhs = jnp.concatenate([v_low[rows, cols], v_high[rows, cols]], axis=0)
            slabs.append(jnp.dot(wcat_ref[p], rhs, preferred_element_type=F32))
        s = jnp.concatenate(slabs, axis=1) + sbias_ref[...]
        a_ref[rows, :] = (u[rows, :] * s).astype(a_ref.dtype)

    def seg_norm(t, gain):
        ms = jnp.dot((t * t).astype(BF16), seg_ref[...], preferred_element_type=F32)
        return (t * lax.rsqrt(ms + EPS) * gain).astype(BF16)

    q_ref[...] = seg_norm(proj[:, c0:c1], qg_ref[...])
    k_ref[...] = seg_norm(proj[:, c1:c2], kg_ref[...])
    v_ref[...] = proj[:, c2:c3].astype(BF16)
    gate_ref[...] = jax.nn.sigmoid(proj[:, c3:] + gbias_ref[...]).astype(gate_ref.dtype)


def _proj(x, g, w_in, gate_bias, sgu_g, wcat, sbias, seg, qg, kg):
    n = x.shape[0]
    row = lambda w: pl.BlockSpec((ROW_TILE, w), lambda i: (i, 0))
    out_w = (A_WIDTH, B_QK, B_QK, B_V, 2 * D_MODEL)
    return pl.pallas_call(
        _proj_body,
        grid=(n // ROW_TILE,),
        in_specs=[row(D_MODEL), _const_spec((1, D_MODEL)), _const_spec((D_MODEL, IN_COLS)),
                  _const_spec((1, 2 * D_MODEL)), _const_spec((1, A_WIDTH)),
                  _const_spec(wcat.shape), _const_spec(sbias.shape), _const_spec(seg.shape),
                  _const_spec((1, B_QK)), _const_spec((1, B_QK))],
        out_specs=[row(w) for w in out_w],
        out_shape=[jax.ShapeDtypeStruct((n, w), BF16) for w in out_w],
        compiler_params=_params(1),
        name="proj",
    )(x, g, w_in, gate_bias, sgu_g, wcat, sbias, seg, qg, kg)


def _bucket_thresholds():
    half = N_BUCKETS // 2
    max_exact = half // 2
    n_log = half - max_exact
    ratio = MAX_DISTANCE // max_exact
    out = []
    for j in range(1, n_log):
        target = max_exact ** n_log * ratio ** j
        n = max_exact
        while n ** n_log < target:
            n += 1
        out.append(n)
    return max_exact, out


def _prep_body(rb_ref, shift_ref, lq1_ref, lk1_ref, lq2_ref, lk2_ref, tiles_ref, tiles_t_ref,
               lam_ref, *, lambda_init):
    t = ATTN_TILE
    half = N_BUCKETS // 2
    max_exact, thresholds = _bucket_thresholds()
    row = lax.broadcasted_iota(jnp.int32, (t, t), 0)
    col = lax.broadcasted_iota(jnp.int32, (t, t), 1)
    shift = shift_ref[0, 0]
    for out_ref, query, key in ((tiles_ref, row, col), (tiles_t_ref, col, row)):
        for hd in range(B_HEADS):
            out_ref[hd, 0] = jnp.full((t, t), rb_ref[half - 1, hd] * LOG2E - shift, F32)
            out_ref[hd, N_BIAS_TILES - 1] = jnp.full(
                (t, t), rb_ref[N_BUCKETS - 1, hd] * LOG2E - shift, F32)
        for d in (-1, 0, 1):
            rel = key - query + d * t
            n = jnp.abs(rel)
            large = jnp.full((t, t), max_exact, jnp.int32)
            for th in thresholds:
                large = large + jnp.where(n >= th, 1, 0)
            bucket = jnp.where(rel > 0, half, 0) + jnp.where(n < max_exact, n, large)
            for hd in range(B_HEADS):
                tile = jnp.zeros((t, t), F32)
                for b in range(N_BUCKETS):
                    tile = jnp.where(bucket == b, rb_ref[b, hd] * LOG2E - shift, tile)
                out_ref[hd, d + N_BIAS_TILES // 2] = tile
    s1 = jnp.sum(lq1_ref[...] * lk1_ref[...], axis=-1, keepdims=True)
    s2 = jnp.sum(lq2_ref[...] * lk2_ref[...], axis=-1, keepdims=True)
    lam = jnp.exp(s1) - jnp.exp(s2) + lambda_init
    lam_ref[...] = jnp.broadcast_to(lam, lam_ref.shape)


def _prep(rel_bias, shift, lq1, lk1, lq2, lk2, lambda_init):
    t = ATTN_TILE
    vec = pl.BlockSpec((1, HEAD_DIM), lambda: (0, 0))
    return pl.pallas_call(
        functools.partial(_prep_body, lambda_init=lambda_init),
        in_specs=[pl.BlockSpec(memory_space=pltpu.SMEM), pl.BlockSpec(memory_space=pltpu.SMEM),
                  vec, vec, vec, vec],
        out_specs=[pl.BlockSpec((B_HEADS, N_BIAS_TILES, t, t), lambda: (0, 0, 0, 0)),
                   pl.BlockSpec((B_HEADS, N_BIAS_TILES, t, t), lambda: (0, 0, 0, 0)),
                   pl.BlockSpec((8, V7X_LANES), lambda: (0, 0))],
        out_shape=[jax.ShapeDtypeStruct((B_HEADS, N_BIAS_TILES, t, t), F32),
                   jax.ShapeDtypeStruct((B_HEADS, N_BIAS_TILES, t, t), F32),
                   jax.ShapeDtypeStruct((8, V7X_LANES), F32)],
        name="prep",
    )(rel_bias, shift, lq1, lk1, lq2, lk2)


def _q_stack(q_ref, c):
    t = ATTN_TILE
    lane = lax.broadcasted_iota(jnp.int32, (t, HEAD_WIDTH), 1)
    first = jnp.where(lane < HEAD_DIM, 1.0, 0.0).astype(BF16)
    q = q_ref[pl.ds(pl.multiple_of(c * t, t), t), :]
    return jnp.concatenate([q * first, q * (1.0 - first)], axis=0)


def _biased_logits(q2, c, j, k_ref, tiles_ref):
    t = ATTN_TILE
    ks = ATTN_KEY_STEP
    bias = jnp.concatenate(
        [tiles_ref[jnp.clip(jt - c, -2, 2) + N_BIAS_TILES // 2]
         for jt in range(j * ks // t, (j + 1) * ks // t)], axis=1)
    s = lax.dot_general(q2, k_ref[j * ks:(j + 1) * ks, :], (((1,), (1,)), ((), ())),
                        preferred_element_type=F32)
    return s + jnp.concatenate([bias, bias], axis=0)


def _store_output(res, c, lam_ref, subg_ref, o_ref, out_scale):
    t = ATTN_TILE
    out = res[:t] - lam_ref[0:1, 0:1] * res[t:]
    o_ref[pl.ds(pl.multiple_of(c * t, t), t), :] = (
        _rms(out, subg_ref[...]) * out_scale).astype(o_ref.dtype)


def _attn_exact_body(lam_ref, q_ref, k_ref, v_ref, tiles_ref, subg_ref, o_ref,
                     s_ref, m_ref, mnext_ref, l_ref, acc_ref, *, out_scale):
    t = ATTN_TILE
    ks = ATTN_KEY_STEP
    nk = k_ref.shape[0] // ks
    nq = q_ref.shape[0] // t
    reps = ks // V7X_LANES

    def logits_step(c, q2, j):
        s = _biased_logits(q2, c, j, k_ref, tiles_ref)
        s_ref[:, j * ks:(j + 1) * ks] = s
        blockmax = s[:, :V7X_LANES]
        for r in range(1, reps):
            blockmax = jnp.maximum(blockmax, s[:, r * V7X_LANES:(r + 1) * V7X_LANES])
        if j == 0:
            mnext_ref[...] = blockmax
        else:
            mnext_ref[...] = jnp.maximum(mnext_ref[...], blockmax)

    def finish_logits():
        m = jnp.max(mnext_ref[...], axis=1, keepdims=True)
        m_ref[...] = jnp.broadcast_to(m, m_ref.shape)

    def prob_step(j, s):
        p = jnp.exp2(s - jnp.concatenate([m_ref[...]] * reps, axis=1))
        psum = p[:, :V7X_LANES]
        for r in range(1, reps):
            psum = psum + p[:, r * V7X_LANES:(r + 1) * V7X_LANES]
        pv = jnp.dot(p.astype(BF16), v_ref[j * ks:(j + 1) * ks, :], preferred_element_type=F32)
        if j == 0:
            l_ref[...] = psum
            acc_ref[...] = pv
        else:
            l_ref[...] += psum
            acc_ref[...] += pv

    def finish_block(c):
        l = jnp.sum(l_ref[...], axis=1, keepdims=True)
        _store_output(acc_ref[...] / l, c, lam_ref, subg_ref, o_ref, out_scale)

    q2 = _q_stack(q_ref, 0)
    for j in range(nk):
        logits_step(0, q2, j)
    finish_logits()

    def body(c, carry):
        q2n = _q_stack(q_ref, c + 1)
        for j in range(nk):
            s = s_ref[:, j * ks:(j + 1) * ks]
            logits_step(c + 1, q2n, j)
            prob_step(j, s)
        finish_block(c)
        finish_logits()
        return carry

    lax.fori_loop(0, nq - 1, body, 0)

    for j in range(nk):
        prob_step(j, s_ref[:, j * ks:(j + 1) * ks])
    finish_block(nq - 1)


def _attn_fast_body(lam_ref, q_ref, k_ref, v_ref, tiles_t_ref, subg_ref, o_ref, pt_ref, vaugt_ref,
                    *, out_scale):
    t = ATTN_TILE
    ks = ATTN_KEY_STEP
    seq = k_ref.shape[0]
    nk = seq // ks
    nq = q_ref.shape[0] // t

    vaugt_ref[:HEAD_WIDTH, :] = v_ref[...].astype(F32).T.astype(BF16)
    vaugt_ref[HEAD_WIDTH:, :] = jnp.ones((ONES_ROWS, seq), BF16)

    def prob_block(c):
        q2 = _q_stack(q_ref, c)
        for j in range(nk):
            keys = slice(j * ks, (j + 1) * ks)
            bias = jnp.concatenate(
                [tiles_t_ref[jnp.clip(jt - c, -2, 2) + N_BIAS_TILES // 2]
                 for jt in range(j * ks // t, (j + 1) * ks // t)], axis=0)
            s = lax.dot_general(k_ref[keys, :], q2, (((1,), (1,)), ((), ())),
                                preferred_element_type=F32)
            pt_ref[keys, :] = jnp.exp2(s + jnp.concatenate([bias, bias], axis=1)).astype(BF16)

    def output_block(c):
        res = jnp.dot(vaugt_ref[...], pt_ref[...], preferred_element_type=F32)
        res = res[:HEAD_WIDTH] / res[HEAD_WIDTH:HEAD_WIDTH + 1]
        out = res[:, :t] - lam_ref[0:1, 0:1] * res[:, t:]
        out = out * lax.rsqrt(jnp.mean(out * out, axis=0, keepdims=True) + EPS)
        o_ref[pl.ds(pl.multiple_of(c * t, t), t), :] = (
            out.T * subg_ref[...] * out_scale).astype(o_ref.dtype)

    prob_block(0)

    def body(c, carry):
        output_block(c - 1)
        prob_block(c)
        return carry

    lax.fori_loop(1, nq, body, 0, unroll=4)
    output_block(nq - 1)


def _attn(lam, q, k, v, tiles, tiles_t, subg, *, out_scale, single_pass):
    b, s, _ = q.shape
    t = ATTN_TILE
    seq = pl.BlockSpec((None, s, HEAD_WIDTH), lambda bi, hi: (bi, 0, hi))
    stack = pltpu.VMEM((2 * t, V7X_LANES), F32)
    if single_pass:
        body, name, bias_tiles = _attn_fast_body, "attn_fast", tiles_t
        scratch = [pltpu.VMEM((s, 2 * t), BF16), pltpu.VMEM((HEAD_WIDTH + ONES_ROWS, s), BF16)]
    else:
        body, name, bias_tiles = _attn_exact_body, "attn_exact", tiles
        scratch = [pltpu.VMEM((2 * t, s), F32), stack, stack, stack, stack]
    return pl.pallas_call(
        functools.partial(body, out_scale=out_scale),
        grid=(b, B_HEADS),
        in_specs=[pl.BlockSpec((8, V7X_LANES), lambda bi, hi: (0, 0)),
                  seq, seq, seq,
                  pl.BlockSpec((None, N_BIAS_TILES, t, t), lambda bi, hi: (hi, 0, 0, 0)),
                  pl.BlockSpec((1, HEAD_WIDTH), lambda bi, hi: (0, 0))],
        out_specs=seq,
        out_shape=jax.ShapeDtypeStruct((b, s, B_V), BF16),
        scratch_shapes=scratch,
        compiler_params=_params(2),
        name=name,
    )(lam, q, k, v, bias_tiles, subg)


def _logit_shift(q_gain, k_gain, rel_bias):
    qk_bound = 1.02 * HEAD_DIM * jnp.max(jnp.abs(q_gain)) * jnp.max(jnp.abs(k_gain))
    bias = rel_bias.astype(F32) * LOG2E
    shift = qk_bound + jnp.max(bias)
    span = 2.0 * qk_bound + jnp.max(bias) - jnp.min(bias)
    return shift.reshape(1, 1), span <= LOGIT_SPAN


def _merge_body(x_ref, a_ref, b_ref, gate_ref, wpa_ref, wpb_ref, wo_ref, o_ref):
    ga = gate_ref[:, :D_MODEL].astype(F32)
    gb = gate_ref[:, D_MODEL:].astype(F32)
    merged = (ga * jnp.dot(a_ref[...], wpa_ref[...], preferred_element_type=F32)
              + gb * jnp.dot(b_ref[...], wpb_ref[...], preferred_element_type=F32))
    o_ref[...] = x_ref[...] + jnp.dot(merged.astype(BF16), wo_ref[...],
                                      preferred_element_type=F32)


def _merge(x, a, b, gates, wpa, wpb, wo):
    n = x.shape[0]
    row = lambda w: pl.BlockSpec((ROW_TILE, w), lambda i: (i, 0))
    return pl.pallas_call(
        _merge_body,
        grid=(n // ROW_TILE,),
        in_specs=[row(D_MODEL), row(A_WIDTH), row(B_V), row(2 * D_MODEL),
                  _const_spec((A_WIDTH, D_MODEL)), _const_spec((B_V, D_MODEL)),
                  _const_spec((D_MODEL, D_MODEL))],
        out_specs=row(D_MODEL),
        out_shape=jax.ShapeDtypeStruct((n, D_MODEL), F32),
        compiler_params=_params(1),
        name="merge",
    )(x, a, b, gates, wpa, wpb, wo)


def _layer(x, lp, tiles, tiles_t, lam, out_scale, single_pass):
    b, s, _ = x.shape
    x0 = x.reshape(b * s, D_MODEL)
    x1 = _ffn(x0, lp["ffn1_norm"], lp["ffn1_w_in"], lp["ffn1_w_out"])
    a, q, k, v, gates = _proj(x1, lp["mix_norm"], lp["w_in"], lp["gate_bias"], lp["sgu_norm"],
                              lp["wcat"], lp["sbias"], lp["seg"], lp["qg"], lp["kg"])
    bo = lax.cond(single_pass,
                  functools.partial(_attn, out_scale=out_scale, single_pass=True),
                  functools.partial(_attn, out_scale=out_scale, single_pass=False),
                  lam, q.reshape(b, s, B_QK), k.reshape(b, s, B_QK), v.reshape(b, s, B_V),
                  tiles, tiles_t, lp["diff_subln"])
    x2 = _merge(x1, a, bo.reshape(b * s, B_V), gates, lp["w_proj_a"], lp["w_proj_b"], lp["w_out"])
    y = _ffn(x2, lp["ffn2_norm"], lp["ffn2_w_in"], lp["ffn2_w_out"], lp["final_norm"])
    return y.reshape(b, s, D_MODEL)


def kernel(x_prompt, x_sample, rel_bias, ffn1_norm, ffn1_w_in, ffn1_w_out, mix_norm, w_in, gate_bias, sgu_norm, sgu_w, sgu_b, q_norm, k_norm, lambda_q1, lambda_k1, lambda_q2, lambda_k2, diff_subln, w_proj_a, w_proj_b, w_out, ffn2_norm, ffn2_w_in, ffn2_w_out, final_norm):
    depth = ffn1_norm.shape[0]
    y_prompt, y_sample = x_prompt, x_sample
    row = lambda p: p.reshape(1, -1).astype(F32)
    for l in range(depth):
        lambda_init = 0.8 - 0.6 * math.exp(-0.3 * l)
        sw = sgu_w[l].astype(BF16)
        lp = {
            "ffn1_norm": row(ffn1_norm[l]), "ffn1_w_in": ffn1_w_in[l].astype(BF16),
            "ffn1_w_out": ffn1_w_out[l].astype(BF16),
            "mix_norm": row(mix_norm[l]), "w_in": w_in[l].astype(BF16),
            "gate_bias": row(gate_bias[l]), "sgu_norm": row(sgu_norm[l]),
            "wcat": jnp.concatenate([sw[0::2], sw[1::2]], axis=-1),
            "sbias": jnp.repeat(sgu_b[l].T.astype(F32), A_GROUP_DIM, axis=1),
            "seg": jnp.kron(jnp.eye(B_QK // HEAD_DIM, dtype=F32),
                            jnp.full((HEAD_DIM, HEAD_DIM), 1.0 / HEAD_DIM, F32)).astype(BF16),
            "qg": jnp.tile(row(q_norm[l]) * (HEAD_DIM ** -0.5 * LOG2E), (1, B_QK // HEAD_DIM)),
            "kg": jnp.tile(row(k_norm[l]), (1, B_QK // HEAD_DIM)),
            "diff_subln": row(diff_subln[l]),
            "w_proj_a": w_proj_a[l].astype(BF16), "w_proj_b": w_proj_b[l].astype(BF16),
            "w_out": w_out[l].astype(BF16),
            "ffn2_norm": row(ffn2_norm[l]), "ffn2_w_in": ffn2_w_in[l].astype(BF16),
            "ffn2_w_out": ffn2_w_out[l].astype(BF16), "final_norm": row(final_norm[l]),
        }
        shift, single_pass = _logit_shift(lp["qg"], lp["kg"], rel_bias)
        tiles, tiles_t, lam = _prep(rel_bias.astype(F32), shift, row(lambda_q1[l]), row(lambda_k1[l]),
                                    row(lambda_q2[l]), row(lambda_k2[l]), lambda_init)
        out_scale = 1.0 - lambda_init
        y_prompt = _layer(y_prompt, lp, tiles, tiles_t, lam, out_scale, single_pass)
        y_sample = _layer(y_sample, lp, tiles, tiles_t, lam, out_scale, single_pass)
    return (y_prompt, y_sample)
```

```python
import functools
import math

import jax
import jax.numpy as jnp
from jax import lax
from jax.experimental import pallas as pl
from jax.experimental.pallas import tpu as pltpu

D_MODEL = 1024
D_FF = 2816
CHUNK = 128
A_WIDTH = 512
A_GROUPS = 8
A_GROUP_DIM = A_WIDTH // A_GROUPS
B_HEADS = 4
HEAD_DIM = 64
HEAD_WIDTH = 2 * HEAD_DIM
B_QK = B_HEADS * HEAD_WIDTH
B_V = B_HEADS * HEAD_WIDTH
N_BUCKETS = 32
MAX_DISTANCE = 128
IN_COLS = 2 * A_WIDTH + 2 * B_QK + B_V + 2 * D_MODEL
EPS = 1e-6

V7X_LANES = 128
V7X_VMEM_LIMIT_BYTES = 56 * 1024 * 1024

ROW_TILE = 512
ATTN_TILE = 256
ATTN_KEY_STEP = 512
LOG2E = math.log2(math.e)
LOGIT_SPAN = 100.0
N_BIAS_TILES = 5
ONES_ROWS = 16

F32 = jnp.float32
BF16 = jnp.bfloat16


def _rms(x, g):
    return x * lax.rsqrt(jnp.mean(x * x, axis=-1, keepdims=True) + EPS) * g


def _const_spec(shape):
    return pl.BlockSpec(shape, lambda *_: (0,) * len(shape), pipeline_mode=pl.Buffered(1))


def _params(n_axes):
    return pltpu.CompilerParams(dimension_semantics=("parallel",) * n_axes,
                                vmem_limit_bytes=V7X_VMEM_LIMIT_BYTES)


def _macaron_half_step(x, g, win_ref, wout_ref):
    h = _rms(x, g).astype(BF16)
    gu = jnp.dot(h, win_ref[...], preferred_element_type=F32)
    gate = gu[:, :D_FF]
    up = gu[:, D_FF:]
    act = (gate * jax.nn.sigmoid(gate) * up).astype(BF16)
    return x + 0.5 * jnp.dot(act, wout_ref[...], preferred_element_type=F32)


def _ffn_body(x_ref, g_ref, win_ref, wout_ref, o_ref):
    o_ref[...] = _macaron_half_step(x_ref[...], g_ref[...], win_ref, wout_ref)


def _ffn(x, g, w_in, w_out):
    n = x.shape[0]
    row = pl.BlockSpec((ROW_TILE, D_MODEL), lambda i: (i, 0))
    return pl.pallas_call(
        _ffn_body,
        grid=(n // ROW_TILE,),
        in_specs=[row, _const_spec((1, D_MODEL)), _const_spec((D_MODEL, 2 * D_FF)),
                  _const_spec((D_FF, D_MODEL))],
        out_specs=row,
        out_shape=jax.ShapeDtypeStruct((n, D_MODEL), F32),
        compiler_params=_params(1),
        name="ffn",
    )(x, g, w_in, w_out)


def _proj_body(x_ref, g_ref, win_ref, gbias_ref, sgug_ref, wcat_ref, sbias_ref, seg_ref,
               qg_ref, kg_ref, a_ref, q_ref, k_ref, v_ref, gate_ref):
    tm = x_ref.shape[0]
    h = _rms(x_ref[...], g_ref[...]).astype(BF16)
    proj = jnp.dot(h, win_ref[...], preferred_element_type=F32)
    c0 = 2 * A_WIDTH
    c1 = c0 + B_QK
    c2 = c1 + B_QK
    c3 = c2 + B_V

    uv = jax.nn.gelu(proj[:, :c0])
    u = uv[:, :A_WIDTH]
    vn = _rms(uv[:, A_WIDTH:], sgug_ref[...])
    lane = lax.broadcasted_iota(jnp.int32, (tm, A_WIDTH), 1)
    low = (lane & (V7X_LANES - 1)) < A_GROUP_DIM
    v_low = jnp.where(low, vn, 0.0).astype(BF16)
    v_high = jnp.where(low, 0.0, vn).astype(BF16)
    for c in range(tm // CHUNK):
        rows = slice(c * CHUNK, (c + 1) * CHUNK)
        slabs = []
        for p in range(A_WIDTH // V7X_LANES):
            cols = slice(p * V7X_LANES, (p + 1) * V7X_LANES)
            rhs = jnp.concatenate([v_low[rows, cols], v_high[rows, cols]], axis=0)
            slabs.append(jnp.dot(wcat_ref[p], rhs, preferred_element_type=F32))
        s = jnp.concatenate(slabs, axis=1) + sbias_ref[...]
        a_ref[rows, :] = (u[rows, :] * s).astype(a_ref.dtype)

    def seg_norm(t, gain):
        ms = jnp.dot((t * t).astype(BF16), seg_ref[...], preferred_element_type=F32)
        return (t * lax.rsqrt(ms + EPS) * gain).astype(BF16)

    q_ref[...] = seg_norm(proj[:, c0:c1], qg_ref[...])
    k_ref[...] = seg_norm(proj[:, c1:c2], kg_ref[...])
    v_ref[...] = proj[:, c2:c3].astype(BF16)
    gate_ref[...] = jax.nn.sigmoid(proj[:, c3:] + gbias_ref[...]).astype(gate_ref.dtype)


def _proj(x, g, w_in, gate_bias, sgu_g, wcat, sbias, seg, qg, kg):
    n = x.shape[0]
    row = lambda w: pl.BlockSpec((ROW_TILE, w), lambda i: (i, 0))
    out_w = (A_WIDTH, B_QK, B_QK, B_V, 2 * D_MODEL)
    return pl.pallas_call(
        _proj_body,
        grid=(n // ROW_TILE,),
        in_specs=[row(D_MODEL), _const_spec((1, D_MODEL)), _const_spec((D_MODEL, IN_COLS)),
                  _const_spec((1, 2 * D_MODEL)), _const_spec((1, A_WIDTH)),
                  _const_spec(wcat.shape), _const_spec(sbias.shape), _const_spec(seg.shape),
                  _const_spec((1, B_QK)), _const_spec((1, B_QK))],
        out_specs=[row(w) for w in out_w],
        out_shape=[jax.ShapeDtypeStruct((n, w), BF16) for w in out_w],
        compiler_params=_params(1),
        name="proj",
    )(x, g, w_in, gate_bias, sgu_g, wcat, sbias, seg, qg, kg)


def _bucket_thresholds():
    half = N_BUCKETS // 2
    max_exact = half // 2
    n_log = half - max_exact
    ratio = MAX_DISTANCE // max_exact
    out = []
    for j in range(1, n_log):
        target = max_exact ** n_log * ratio ** j
        n = max_exact
        while n ** n_log < target:
            n += 1
        out.append(n)
    return max_exact, out


def _prep_body(rb_ref, shift_ref, lq1_ref, lk1_ref, lq2_ref, lk2_ref, tiles_ref, tiles_t_ref,
               lam_ref, *, lambda_init):
    t = ATTN_TILE
    half = N_BUCKETS // 2
    max_exact, thresholds = _bucket_thresholds()
    row = lax.broadcasted_iota(jnp.int32, (t, t), 0)
    col = lax.broadcasted_iota(jnp.int32, (t, t), 1)
    shift = shift_ref[0, 0]
    for out_ref, query, key in ((tiles_ref, row, col), (tiles_t_ref, col, row)):
        for hd in range(B_HEADS):
            out_ref[hd, 0] = jnp.full((t, t), rb_ref[half - 1, hd] * LOG2E - shift, F32)
            out_ref[hd, N_BIAS_TILES - 1] = jnp.full(
                (t, t), rb_ref[N_BUCKETS - 1, hd] * LOG2E - shift, F32)
        for d in (-1, 0, 1):
            rel = key - query + d * t
            n = jnp.abs(rel)
            large = jnp.full((t, t), max_exact, jnp.int32)
            for th in thresholds:
                large = large + jnp.where(n >= th, 1, 0)
            bucket = jnp.where(rel > 0, half, 0) + jnp.where(n < max_exact, n, large)
            for hd in range(B_HEADS):
                tile = jnp.zeros((t, t), F32)
                for b in range(N_BUCKETS):
                    tile = jnp.where(bucket == b, rb_ref[b, hd] * LOG2E - shift, tile)
                out_ref[hd, d + N_BIAS_TILES // 2] = tile
    s1 = jnp.sum(lq1_ref[...] * lk1_ref[...], axis=-1, keepdims=True)
    s2 = jnp.sum(lq2_ref[...] * lk2_ref[...], axis=-1, keepdims=True)
    lam = jnp.exp(s1) - jnp.exp(s2) + lambda_init
    lam_ref[...] = jnp.broadcast_to(lam, lam_ref.shape)


def _prep(rel_bias, shift, lq1, lk1, lq2, lk2, lambda_init):
    t = ATTN_TILE
    vec = pl.BlockSpec((1, HEAD_DIM), lambda: (0, 0))
    return pl.pallas_call(
        functools.partial(_prep_body, lambda_init=lambda_init),
        in_specs=[pl.BlockSpec(memory_space=pltpu.SMEM), pl.BlockSpec(memory_space=pltpu.SMEM),
                  vec, vec, vec, vec],
        out_specs=[pl.BlockSpec((B_HEADS, N_BIAS_TILES, t, t), lambda: (0, 0, 0, 0)),
                   pl.BlockSpec((B_HEADS, N_BIAS_TILES, t, t), lambda: (0, 0, 0, 0)),
                   pl.BlockSpec((8, V7X_LANES), lambda: (0, 0))],
        out_shape=[jax.ShapeDtypeStruct((B_HEADS, N_BIAS_TILES, t, t), F32),
                   jax.ShapeDtypeStruct((B_HEADS, N_BIAS_TILES, t, t), F32),
                   jax.ShapeDtypeStruct((8, V7X_LANES), F32)],
        name="prep",
    )(rel_bias, shift, lq1, lk1, lq2, lk2)


def _q_stack(q_ref, c):
    t = ATTN_TILE
    lane = lax.broadcasted_iota(jnp.int32, (t, HEAD_WIDTH), 1)
    first = jnp.where(lane < HEAD_DIM, 1.0, 0.0).astype(BF16)
    q = q_ref[pl.ds(pl.multiple_of(c * t, t), t), :]
    return jnp.concatenate([q * first, q * (1.0 - first)], axis=0)


def _biased_logits(q2, c, j, k_ref, tiles_ref):
    t = ATTN_TILE
    ks = ATTN_KEY_STEP
    bias = jnp.concatenate(
        [tiles_ref[jnp.clip(jt - c, -2, 2) + N_BIAS_TILES // 2]
         for jt in range(j * ks // t, (j + 1) * ks // t)], axis=1)
    s = lax.dot_general(q2, k_ref[j * ks:(j + 1) * ks, :], (((1,), (1,)), ((), ())),
                        preferred_element_type=F32)
    return s + jnp.concatenate([bias, bias], axis=0)


def _store_output(res, c, lam_ref, subg_ref, o_ref, out_scale):
    t = ATTN_TILE
    out = res[:t] - lam_ref[0:1, 0:1] * res[t:]
    o_ref[pl.ds(pl.multiple_of(c * t, t), t), :] = (
        _rms(out, subg_ref[...]) * out_scale).astype(o_ref.dtype)


def _attn_exact_body(lam_ref, q_ref, k_ref, v_ref, tiles_ref, subg_ref, o_ref,
                     s_ref, m_ref, mnext_ref, l_ref, acc_ref, *, out_scale):
    t = ATTN_TILE
    ks = ATTN_KEY_STEP
    nk = k_ref.shape[0] // ks
    nq = q_ref.shape[0] // t
    reps = ks // V7X_LANES

    def logits_step(c, q2, j):
        s = _biased_logits(q2, c, j, k_ref, tiles_ref)
        s_ref[:, j * ks:(j + 1) * ks] = s
        blockmax = s[:, :V7X_LANES]
        for r in range(1, reps):
            blockmax = jnp.maximum(blockmax, s[:, r * V7X_LANES:(r + 1) * V7X_LANES])
        if j == 0:
            mnext_ref[...] = blockmax
        else:
            mnext_ref[...] = jnp.maximum(mnext_ref[...], blockmax)

    def finish_logits():
        m = jnp.max(mnext_ref[...], axis=1, keepdims=True)
        m_ref[...] = jnp.broadcast_to(m, m_ref.shape)

    def prob_step(j, s):
        p = jnp.exp2(s - jnp.concatenate([m_ref[...]] * reps, axis=1))
        psum = p[:, :V7X_LANES]
        for r in range(1, reps):
            psum = psum + p[:, r * V7X_LANES:(r + 1) * V7X_LANES]
        pv = jnp.dot(p.astype(BF16), v_ref[j * ks:(j + 1) * ks, :], preferred_element_type=F32)
        if j == 0:
            l_ref[...] = psum
            acc_ref[...] = pv
        else:
            l_ref[...] += psum
            acc_ref[...] += pv

    def finish_block(c):
        l = jnp.sum(l_ref[...], axis=1, keepdims=True)
        _store_output(acc_ref[...] / l, c, lam_ref, subg_ref, o_ref, out_scale)

    q2 = _q_stack(q_ref, 0)
    for j in range(nk):
        logits_step(0, q2, j)
    finish_logits()

    def body(c, carry):
        q2n = _q_stack(q_ref, c + 1)
        for j in range(nk):
            s = s_ref[:, j * ks:(j + 1) * ks]
            logits_step(c + 1, q2n, j)
            prob_step(j, s)
        finish_block(c)
        finish_logits()
        return carry

    lax.fori_loop(0, nq - 1, body, 0)

    for j in range(nk):
        prob_step(j, s_ref[:, j * ks:(j + 1) * ks])
    finish_block(nq - 1)


def _attn_fast_body(lam_ref, q_ref, k_ref, v_ref, tiles_t_ref, subg_ref, o_ref, pt_ref, vaugt_ref,
                    *, out_scale):
    t = ATTN_TILE
    ks = ATTN_KEY_STEP
    seq = k_ref.shape[0]
    nk = seq // ks
    nq = q_ref.shape[0] // t

    vaugt_ref[:HEAD_WIDTH, :] = v_ref[...].astype(F32).T.astype(BF16)
    vaugt_ref[HEAD_WIDTH:, :] = jnp.ones((ONES_ROWS, seq), BF16)

    def prob_block(c):
        q2 = _q_stack(q_ref, c)
        for j in range(nk):
            keys = slice(j * ks, (j + 1) * ks)
            bias = jnp.concatenate(
                [tiles_t_ref[jnp.clip(jt - c, -2, 2) + N_BIAS_TILES // 2]
                 for jt in range(j * ks // t, (j + 1) * ks // t)], axis=0)
            s = lax.dot_general(k_ref[keys, :], q2, (((1,), (1,)), ((), ())),
                                preferred_element_type=F32)
            pt_ref[keys, :] = jnp.exp2(s + jnp.concatenate([bias, bias], axis=1)).astype(BF16)

    def output_block(c):
        res = jnp.dot(vaugt_ref[...], pt_ref[...], preferred_element_type=F32)
        res = res[:HEAD_WIDTH] / res[HEAD_WIDTH:HEAD_WIDTH + 1]
        out = res[:, :t] - lam_ref[0:1, 0:1] * res[:, t:]
        out = out * lax.rsqrt(jnp.mean(out * out, axis=0, keepdims=True) + EPS)
        o_ref[pl.ds(pl.multiple_of(c * t, t), t), :] = (
            out.T * subg_ref[...] * out_scale).astype(o_ref.dtype)

    prob_block(0)

    def body(c, carry):
        output_block(c - 1)
        prob_block(c)
        return carry

    lax.fori_loop(1, nq, body, 0, unroll=4)
    output_block(nq - 1)


def _attn(lam, q, k, v, tiles, tiles_t, subg, *, out_scale, single_pass):
    b, s, _ = q.shape
    t = ATTN_TILE
    seq = pl.BlockSpec((None, s, HEAD_WIDTH), lambda bi, hi: (bi, 0, hi))
    stack = pltpu.VMEM((2 * t, V7X_LANES), F32)
    if single_pass:
        body, name, bias_tiles = _attn_fast_body, "attn_fast", tiles_t
        scratch = [pltpu.VMEM((s, 2 * t), BF16), pltpu.VMEM((HEAD_WIDTH + ONES_ROWS, s), BF16)]
    else:
        body, name, bias_tiles = _attn_exact_body, "attn_exact", tiles
        scratch = [pltpu.VMEM((2 * t, s), F32), stack, stack, stack, stack]
    return pl.pallas_call(
        functools.partial(body, out_scale=out_scale),
        grid=(b, B_HEADS),
        in_specs=[pl.BlockSpec((8, V7X_LANES), lambda bi, hi: (0, 0)),
                  seq, seq, seq,
                  pl.BlockSpec((None, N_BIAS_TILES, t, t), lambda bi, hi: (hi, 0, 0, 0)),
                  pl.BlockSpec((1, HEAD_WIDTH), lambda bi, hi: (0, 0))],
        out_specs=seq,
        out_shape=jax.ShapeDtypeStruct((b, s, B_V), BF16),
        scratch_shapes=scratch,
        compiler_params=_params(2),
        name=name,
    )(lam, q, k, v, bias_tiles, subg)


def _logit_shift(q_gain, k_gain, rel_bias):
    qk_bound = 1.02 * HEAD_DIM * jnp.max(jnp.abs(q_gain)) * jnp.max(jnp.abs(k_gain))
    bias = rel_bias.astype(F32) * LOG2E
    shift = qk_bound + jnp.max(bias)
    span = 2.0 * qk_bound + jnp.max(bias) - jnp.min(bias)
    return shift.reshape(1, 1), span <= LOGIT_SPAN


def _merge_ffn_body(x_ref, a_ref, b_ref, gate_ref, wpa_ref, wpb_ref, wo_ref, g_ref, win_ref,
                    wout_ref, fg_ref, o_ref):
    ga = gate_ref[:, :D_MODEL].astype(F32)
    gb = gate_ref[:, D_MODEL:].astype(F32)
    merged = (ga * jnp.dot(a_ref[...], wpa_ref[...], preferred_element_type=F32)
              + gb * jnp.dot(b_ref[...], wpb_ref[...], preferred_element_type=F32))
    x = x_ref[...] + jnp.dot(merged.astype(BF16), wo_ref[...], preferred_element_type=F32)
    o_ref[...] = _rms(_macaron_half_step(x, g_ref[...], win_ref, wout_ref), fg_ref[...])


def _merge_ffn(x, a, b, gates, wpa, wpb, wo, g, w_in, w_out, final_g):
    n = x.shape[0]
    row = lambda w: pl.BlockSpec((ROW_TILE, w), lambda i: (i, 0))
    return pl.pallas_call(
        _merge_ffn_body,
        grid=(n // ROW_TILE,),
        in_specs=[row(D_MODEL), row(A_WIDTH), row(B_V), row(2 * D_MODEL),
                  _const_spec((A_WIDTH, D_MODEL)), _const_spec((B_V, D_MODEL)),
                  _const_spec((D_MODEL, D_MODEL)), _const_spec((1, D_MODEL)),
                  _const_spec((D_MODEL, 2 * D_FF)), _const_spec((D_FF, D_MODEL)),
                  _const_spec((1, D_MODEL))],
        out_specs=row(D_MODEL),
        out_shape=jax.ShapeDtypeStruct((n, D_MODEL), F32),
        compiler_params=_params(1),
        name="merge_ffn",
    )(x, a, b, gates, wpa, wpb, wo, g, w_in, w_out, final_g)


def _layer(x, lp, tiles, tiles_t, lam, out_scale, single_pass):
    b, s, _ = x.shape
    x0 = x.reshape(b * s, D_MODEL)
    x1 = _ffn(x0, lp["ffn1_norm"], lp["ffn1_w_in"], lp["ffn1_w_out"])
    a, q, k, v, gates = _proj(x1, lp["mix_norm"], lp["w_in"], lp["gate_bias"], lp["sgu_norm"],
                              lp["wcat"], lp["sbias"], lp["seg"], lp["qg"], lp["kg"])
    bo = lax.cond(single_pass,
                  functools.partial(_attn, out_scale=out_scale, single_pass=True),
                  functools.partial(_attn, out_scale=out_scale, single_pass=False),
                  lam, q.reshape(b, s, B_QK), k.reshape(b, s, B_QK), v.reshape(b, s, B_V),
                  tiles, tiles_t, lp["diff_subln"])
    y = _merge_ffn(x1, a, bo.reshape(b * s, B_V), gates, lp["w_proj_a"], lp["w_proj_b"], lp["w_out"],
                   lp["ffn2_norm"], lp["ffn2_w_in"], lp["ffn2_w_out"], lp["final_norm"])
    return y.reshape(b, s, D_MODEL)


def kernel(x_prompt, x_sample, rel_bias, ffn1_norm, ffn1_w_in, ffn1_w_out, mix_norm, w_in, gate_bias, sgu_norm, sgu_w, sgu_b, q_norm, k_norm, lambda_q1, lambda_k1, lambda_q2, lambda_k2, diff_subln, w_proj_a, w_proj_b, w_out, ffn2_norm, ffn2_w_in, ffn2_w_out, final_norm):
    depth = ffn1_norm.shape[0]
    y_prompt, y_sample = x_prompt, x_sample
    row = lambda p: p.reshape(1, -1).astype(F32)
    for l in range(depth):
        lambda_init = 0.8 - 0.6 * math.exp(-0.3 * l)
        sw = sgu_w[l].astype(BF16)
        lp = {
            "ffn1_norm": row(ffn1_norm[l]), "ffn1_w_in": ffn1_w_in[l].astype(BF16),
            "ffn1_w_out": ffn1_w_out[l].astype(BF16),
            "mix_norm": row(mix_norm[l]), "w_in": w_in[l].astype(BF16),
            "gate_bias": row(gate_bias[l]), "sgu_norm": row(sgu_norm[l]),
            "wcat": jnp.concatenate([sw[0::2], sw[1::2]], axis=-1),
            "sbias": jnp.repeat(sgu_b[l].T.astype(F32), A_GROUP_DIM, axis=1),
            "seg": jnp.kron(jnp.eye(B_QK // HEAD_DIM, dtype=F32),
                            jnp.full((HEAD_DIM, HEAD_DIM), 1.0 / HEAD_DIM, F32)).astype(BF16),
            "qg": jnp.tile(row(q_norm[l]) * (HEAD_DIM ** -0.5 * LOG2E), (1, B_QK // HEAD_DIM)),
            "kg": jnp.tile(row(k_norm[l]), (1, B_QK // HEAD_DIM)),
            "diff_subln": row(diff_subln[l]),
            "w_proj_a": w_proj_a[l].astype(BF16), "w_proj_b": w_proj_b[l].astype(BF16),
            "w_out": w_out[l].astype(BF16),
            "ffn2_norm": row(ffn2_norm[l]), "ffn2_w_in": ffn2_w_in[l].astype(BF16),
            "ffn2_w_out": ffn2_w_out[l].astype(BF16), "final_norm": row(final_norm[l]),
        }
        shift, single_pass = _logit_shift(lp["qg"], lp["kg"], rel_bias)
        tiles, tiles_t, lam = _prep(rel_bias.astype(F32), shift, row(lambda_q1[l]), row(lambda_k1[l]),
                                    row(lambda_q2[l]), row(lambda_k2[l]), lambda_init)
        out_scale = 1.0 - lambda_init
        y_prompt = _layer(y_prompt, lp, tiles, tiles_t, lam, out_scale, single_pass)
        y_sample = _layer(y_sample, lp, tiles, tiles_t, lam, out_scale, single_pass)
    return (y_prompt, y_sample)
```

```python
import functools
import math

import jax
import jax.numpy as jnp
from jax import lax
from jax.experimental import pallas as pl
from jax.experimental.pallas import tpu as pltpu

D_MODEL = 1024
D_FF = 2816
CHUNK = 128
A_WIDTH = 512
A_GROUPS = 8
A_GROUP_DIM = A_WIDTH // A_GROUPS
B_HEADS = 4
HEAD_DIM = 64
HEAD_WIDTH = 2 * HEAD_DIM
B_QK = B_HEADS * HEAD_WIDTH
B_V = B_HEADS * HEAD_WIDTH
N_BUCKETS = 32
MAX_DISTANCE = 128
IN_COLS = 2 * A_WIDTH + 2 * B_QK + B_V + 2 * D_MODEL
EPS = 1e-6

V7X_LANES = 128
V7X_MXU_DIM = 256
V7X_VMEM_LIMIT_BYTES = 56 * 1024 * 1024

ROW_TILE = 512
ATTN_TILE = 256
ATTN_KEY_STEP = 512
LOG2E = math.log2(math.e)
LOGIT_SPAN = 100.0
N_BIAS_TILES = 5
ONES_ROWS = 16

F32 = jnp.float32
BF16 = jnp.bfloat16


def _rms(x, g):
    return x * lax.rsqrt(jnp.mean(x * x, axis=-1, keepdims=True) + EPS) * g


def _const_spec(shape):
    return pl.BlockSpec(shape, lambda *_: (0,) * len(shape), pipeline_mode=pl.Buffered(1))


def _params(n_axes):
    return pltpu.CompilerParams(dimension_semantics=("parallel",) * n_axes,
                                vmem_limit_bytes=V7X_VMEM_LIMIT_BYTES)


def _macaron_half_step(x, g, win_ref, wout_ref):
    h = _rms(x, g).astype(BF16)
    gu = jnp.dot(h, win_ref[...], preferred_element_type=F32)
    gate = gu[:, :D_FF]
    up = gu[:, D_FF:]
    act = (gate * jax.nn.sigmoid(gate) * up).astype(BF16)
    return x + 0.5 * jnp.dot(act, wout_ref[...], preferred_element_type=F32)


def _ffn_body(x_ref, g_ref, win_ref, wout_ref, o_ref):
    o_ref[...] = _macaron_half_step(x_ref[...], g_ref[...], win_ref, wout_ref)


def _ffn(x, g, w_in, w_out):
    n = x.shape[0]
    row = pl.BlockSpec((ROW_TILE, D_MODEL), lambda i: (i, 0))
    return pl.pallas_call(
        _ffn_body,
        grid=(n // ROW_TILE,),
        in_specs=[row, _const_spec((1, D_MODEL)), _const_spec((D_MODEL, 2 * D_FF)),
                  _const_spec((D_FF, D_MODEL))],
        out_specs=row,
        out_shape=jax.ShapeDtypeStruct((n, D_MODEL), F32),
        compiler_params=_params(1),
        name="ffn",
    )(x, g, w_in, w_out)


def _proj_body(x_ref, g_ref, win_ref, gbias_ref, sgug_ref, wcat_ref, sbias_ref, seg_ref,
               qg_ref, kg_ref, a_ref, q_ref, k_ref, v_ref, gate_ref):
    tm = x_ref.shape[0]
    h = _rms(x_ref[...], g_ref[...]).astype(BF16)
    proj = jnp.dot(h, win_ref[...], preferred_element_type=F32)
    c0 = 2 * A_WIDTH
    c1 = c0 + B_QK
    c2 = c1 + B_QK
    c3 = c2 + B_V

    uv = jax.nn.gelu(proj[:, :c0])
    u = uv[:, :A_WIDTH]
    vn = _rms(uv[:, A_WIDTH:], sgug_ref[...])
    lane = lax.broadcasted_iota(jnp.int32, (tm, A_WIDTH), 1)
    low = (lane & (V7X_LANES - 1)) < A_GROUP_DIM
    v_low = jnp.where(low, vn, 0.0).astype(BF16)
    v_high = jnp.where(low, 0.0, vn).astype(BF16)
    for c in range(0, tm // CHUNK, 2):
        pair = [slice((c + i) * CHUNK, (c + i + 1) * CHUNK) for i in range(2)]
        slabs = []
        for p in range(A_WIDTH // V7X_LANES):
            cols = slice(p * V7X_LANES, (p + 1) * V7X_LANES)
            rhs = jnp.concatenate(
                [jnp.concatenate([v_low[rows, cols], v_high[rows, cols]], axis=0)
                 for rows in pair], axis=1)
            slabs.append(jnp.dot(wcat_ref[p], rhs, preferred_element_type=F32))
        for i, rows in enumerate(pair):
            s = jnp.concatenate([slab[:, i * V7X_LANES:(i + 1) * V7X_LANES] for slab in slabs],
                                axis=1) + sbias_ref[...]
            a_ref[rows, :] = (u[rows, :] * s).astype(a_ref.dtype)

    def seg_norm(t, gain):
        sq = (t * t).astype(BF16)
        half = seg_ref.shape[0]
        ms = jnp.concatenate(
            [jnp.dot(sq[:, i * half:(i + 1) * half], seg_ref[...], preferred_element_type=F32)
             for i in range(B_QK // half)], axis=1)
        return (t * lax.rsqrt(ms + EPS) * gain).astype(BF16)

    q_ref[...] = seg_norm(proj[:, c0:c1], qg_ref[...])
    k_ref[...] = seg_norm(proj[:, c1:c2], kg_ref[...])
    v_ref[...] = proj[:, c2:c3].astype(BF16)
    gate_ref[...] = jax.nn.sigmoid(proj[:, c3:] + gbias_ref[...]).astype(gate_ref.dtype)


def _proj(x, g, w_in, gate_bias, sgu_g, wcat, sbias, seg, qg, kg):
    n = x.shape[0]
    row = lambda w: pl.BlockSpec((ROW_TILE, w), lambda i: (i, 0))
    out_w = (A_WIDTH, B_QK, B_QK, B_V, 2 * D_MODEL)
    return pl.pallas_call(
        _proj_body,
        grid=(n // ROW_TILE,),
        in_specs=[row(D_MODEL), _const_spec((1, D_MODEL)), _const_spec((D_MODEL, IN_COLS)),
                  _const_spec((1, 2 * D_MODEL)), _const_spec((1, A_WIDTH)),
                  _const_spec(wcat.shape), _const_spec(sbias.shape), _const_spec(seg.shape),
                  _const_spec((1, B_QK)), _const_spec((1, B_QK))],
        out_specs=[row(w) for w in out_w],
        out_shape=[jax.ShapeDtypeStruct((n, w), BF16) for w in out_w],
        compiler_params=_params(1),
        name="proj",
    )(x, g, w_in, gate_bias, sgu_g, wcat, sbias, seg, qg, kg)


def _bucket_thresholds():
    half = N_BUCKETS // 2
    max_exact = half // 2
    n_log = half - max_exact
    ratio = MAX_DISTANCE // max_exact
    out = []
    for j in range(1, n_log):
        target = max_exact ** n_log * ratio ** j
        n = max_exact
        while n ** n_log < target:
            n += 1
        out.append(n)
    return max_exact, out


def _prep_body(rb_ref, shift_ref, lq1_ref, lk1_ref, lq2_ref, lk2_ref, tiles_ref, tiles_t_ref,
               lam_ref, *, lambda_init):
    t = ATTN_TILE
    half = N_BUCKETS // 2
    max_exact, thresholds = _bucket_thresholds()
    row = lax.broadcasted_iota(jnp.int32, (t, t), 0)
    col = lax.broadcasted_iota(jnp.int32, (t, t), 1)
    shift = shift_ref[0, 0]
    for out_ref, query, key in ((tiles_ref, row, col), (tiles_t_ref, col, row)):
        for hd in range(B_HEADS):
            out_ref[hd, 0] = jnp.full((t, t), rb_ref[half - 1, hd] * LOG2E - shift, F32)
            out_ref[hd, N_BIAS_TILES - 1] = jnp.full(
                (t, t), rb_ref[N_BUCKETS - 1, hd] * LOG2E - shift, F32)
        for d in (-1, 0, 1):
            rel = key - query + d * t
            n = jnp.abs(rel)
            large = jnp.full((t, t), max_exact, jnp.int32)
            for th in thresholds:
                large = large + jnp.where(n >= th, 1, 0)
            bucket = jnp.where(rel > 0, half, 0) + jnp.where(n < max_exact, n, large)
            for hd in range(B_HEADS):
                tile = jnp.zeros((t, t), F32)
                for b in range(N_BUCKETS):
                    tile = jnp.where(bucket == b, rb_ref[b, hd] * LOG2E - shift, tile)
                out_ref[hd, d + N_BIAS_TILES // 2] = tile
    s1 = jnp.sum(lq1_ref[...] * lk1_ref[...], axis=-1, keepdims=True)
    s2 = jnp.sum(lq2_ref[...] * lk2_ref[...], axis=-1, keepdims=True)
    lam = jnp.exp(s1) - jnp.exp(s2) + lambda_init
    lam_ref[...] = jnp.broadcast_to(lam, lam_ref.shape)


def _prep(rel_bias, shift, lq1, lk1, lq2, lk2, lambda_init):
    t = ATTN_TILE
    vec = pl.BlockSpec((1, HEAD_DIM), lambda: (0, 0))
    return pl.pallas_call(
        functools.partial(_prep_body, lambda_init=lambda_init),
        in_specs=[pl.BlockSpec(memory_space=pltpu.SMEM), pl.BlockSpec(memory_space=pltpu.SMEM),
                  vec, vec, vec, vec],
        out_specs=[pl.BlockSpec((B_HEADS, N_BIAS_TILES, t, t), lambda: (0, 0, 0, 0)),
                   pl.BlockSpec((B_HEADS, N_BIAS_TILES, t, t), lambda: (0, 0, 0, 0)),
                   pl.BlockSpec((8, V7X_LANES), lambda: (0, 0))],
        out_shape=[jax.ShapeDtypeStruct((B_HEADS, N_BIAS_TILES, t, t), F32),
                   jax.ShapeDtypeStruct((B_HEADS, N_BIAS_TILES, t, t), F32),
                   jax.ShapeDtypeStruct((8, V7X_LANES), F32)],
        name="prep",
    )(rel_bias, shift, lq1, lk1, lq2, lk2)


def _q_stack(q_ref, c):
    t = ATTN_TILE
    lane = lax.broadcasted_iota(jnp.int32, (t, HEAD_WIDTH), 1)
    first = jnp.where(lane < HEAD_DIM, 1.0, 0.0).astype(BF16)
    q = q_ref[pl.ds(pl.multiple_of(c * t, t), t), :]
    return jnp.concatenate([q * first, q * (1.0 - first)], axis=0)


def _biased_logits(q2, c, j, k_ref, tiles_ref):
    t = ATTN_TILE
    ks = ATTN_KEY_STEP
    bias = jnp.concatenate(
        [tiles_ref[jnp.clip(jt - c, -2, 2) + N_BIAS_TILES // 2]
         for jt in range(j * ks // t, (j + 1) * ks // t)], axis=1)
    s = lax.dot_general(q2, k_ref[j * ks:(j + 1) * ks, :], (((1,), (1,)), ((), ())),
                        preferred_element_type=F32)
    return s + jnp.concatenate([bias, bias], axis=0)


def _store_output(res, c, lam_ref, subg_ref, o_ref, out_scale):
    t = ATTN_TILE
    out = res[:t] - lam_ref[0:1, 0:1] * res[t:]
    o_ref[pl.ds(pl.multiple_of(c * t, t), t), :] = (
        _rms(out, subg_ref[...]) * out_scale).astype(o_ref.dtype)


def _attn_exact_body(lam_ref, q_ref, k_ref, v_ref, tiles_ref, subg_ref, o_ref,
                     s_ref, m_ref, mnext_ref, l_ref, acc_ref, *, out_scale):
    t = ATTN_TILE
    ks = ATTN_KEY_STEP
    nk = k_ref.shape[0] // ks
    nq = q_ref.shape[0] // t
    reps = ks // V7X_LANES

    def logits_step(c, q2, j):
        s = _biased_logits(q2, c, j, k_ref, tiles_ref)
        s_ref[:, j * ks:(j + 1) * ks] = s
        blockmax = s[:, :V7X_LANES]
        for r in range(1, reps):
            blockmax = jnp.maximum(blockmax, s[:, r * V7X_LANES:(r + 1) * V7X_LANES])
        if j == 0:
            mnext_ref[...] = blockmax
        else:
            mnext_ref[...] = jnp.maximum(mnext_ref[...], blockmax)

    def finish_logits():
        m = jnp.max(mnext_ref[...], axis=1, keepdims=True)
        m_ref[...] = jnp.broadcast_to(m, m_ref.shape)

    def prob_step(j, s):
        p = jnp.exp2(s - jnp.concatenate([m_ref[...]] * reps, axis=1))
        psum = p[:, :V7X_LANES]
        for r in range(1, reps):
            psum = psum + p[:, r * V7X_LANES:(r + 1) * V7X_LANES]
        pv = jnp.dot(p.astype(BF16), v_ref[j * ks:(j + 1) * ks, :], preferred_element_type=F32)
        if j == 0:
            l_ref[...] = psum
            acc_ref[...] = pv
        else:
            l_ref[...] += psum
            acc_ref[...] += pv

    def finish_block(c):
        l = jnp.sum(l_ref[...], axis=1, keepdims=True)
        _store_output(acc_ref[...] / l, c, lam_ref, subg_ref, o_ref, out_scale)

    q2 = _q_stack(q_ref, 0)
    for j in range(nk):
        logits_step(0, q2, j)
    finish_logits()

    def body(c, carry):
        q2n = _q_stack(q_ref, c + 1)
        for j in range(nk):
            s = s_ref[:, j * ks:(j + 1) * ks]
            logits_step(c + 1, q2n, j)
            prob_step(j, s)
        finish_block(c)
        finish_logits()
        return carry

    lax.fori_loop(0, nq - 1, body, 0)

    for j in range(nk):
        prob_step(j, s_ref[:, j * ks:(j + 1) * ks])
    finish_block(nq - 1)


def _attn_fast_body(lam_ref, q_ref, k_ref, v_ref, tiles_t_ref, subg_ref, o_ref, pt_ref, vaugt_ref,
                    *, out_scale):
    t = ATTN_TILE
    ks = ATTN_KEY_STEP
    seq = k_ref.shape[0]
    nk = seq // ks
    nq = q_ref.shape[0] // t

    vaugt_ref[:HEAD_WIDTH, :] = v_ref[...].astype(F32).T.astype(BF16)
    vaugt_ref[HEAD_WIDTH:, :] = jnp.ones((ONES_ROWS, seq), BF16)

    def prob_block(c):
        q2 = _q_stack(q_ref, c)
        for j in range(nk):
            keys = slice(j * ks, (j + 1) * ks)
            bias = jnp.concatenate(
                [tiles_t_ref[jnp.clip(jt - c, -2, 2) + N_BIAS_TILES // 2]
                 for jt in range(j * ks // t, (j + 1) * ks // t)], axis=0)
            s = lax.dot_general(k_ref[keys, :], q2, (((1,), (1,)), ((), ())),
                                preferred_element_type=F32)
            pt_ref[keys, :] = jnp.exp2(s + jnp.concatenate([bias, bias], axis=1)).astype(BF16)

    def output_block(c):
        res = jnp.dot(vaugt_ref[...], pt_ref[...], preferred_element_type=F32)
        res = res[:HEAD_WIDTH] / res[HEAD_WIDTH:HEAD_WIDTH + 1]
        out = res[:, :t] - lam_ref[0:1, 0:1] * res[:, t:]
        out = out * lax.rsqrt(jnp.mean(out * out, axis=0, keepdims=True) + EPS)
        o_ref[pl.ds(pl.multiple_of(c * t, t), t), :] = (
            out.T * subg_ref[...] * out_scale).astype(o_ref.dtype)

    prob_block(0)

    def body(c, carry):
        output_block(c - 1)
        prob_block(c)
        return carry

    lax.fori_loop(1, nq, body, 0, unroll=4)
    output_block(nq - 1)


def _attn(lam, q, k, v, tiles, tiles_t, subg, *, out_scale, single_pass):
    b, s, _ = q.shape
    t = ATTN_TILE
    seq = pl.BlockSpec((None, s, HEAD_WIDTH), lambda bi, hi: (bi, 0, hi))
    stack = pltpu.VMEM((2 * t, V7X_LANES), F32)
    if single_pass:
        body, name, bias_tiles = _attn_fast_body, "attn_fast", tiles_t
        scratch = [pltpu.VMEM((s, 2 * t), BF16), pltpu.VMEM((HEAD_WIDTH + ONES_ROWS, s), BF16)]
    else:
        body, name, bias_tiles = _attn_exact_body, "attn_exact", tiles
        scratch = [pltpu.VMEM((2 * t, s), F32), stack, stack, stack, stack]
    return pl.pallas_call(
        functools.partial(body, out_scale=out_scale),
        grid=(b, B_HEADS),
        in_specs=[pl.BlockSpec((8, V7X_LANES), lambda bi, hi: (0, 0)),
                  seq, seq, seq,
                  pl.BlockSpec((None, N_BIAS_TILES, t, t), lambda bi, hi: (hi, 0, 0, 0)),
                  pl.BlockSpec((1, HEAD_WIDTH), lambda bi, hi: (0, 0))],
        out_specs=seq,
        out_shape=jax.ShapeDtypeStruct((b, s, B_V), BF16),
        scratch_shapes=scratch,
        compiler_params=_params(2),
        name=name,
    )(lam, q, k, v, bias_tiles, subg)


def _logit_shift(q_gain, k_gain, rel_bias):
    qk_bound = 1.02 * HEAD_DIM * jnp.max(jnp.abs(q_gain)) * jnp.max(jnp.abs(k_gain))
    bias = rel_bias.astype(F32) * LOG2E
    shift = qk_bound + jnp.max(bias)
    span = 2.0 * qk_bound + jnp.max(bias) - jnp.min(bias)
    return shift.reshape(1, 1), span <= LOGIT_SPAN


def _merge_ffn_body(x_ref, a_ref, b_ref, gate_ref, wpa_ref, wpb_ref, wo_ref, g_ref, win_ref,
                    wout_ref, fg_ref, o_ref):
    ga = gate_ref[:, :D_MODEL].astype(F32)
    gb = gate_ref[:, D_MODEL:].astype(F32)
    merged = (ga * jnp.dot(a_ref[...], wpa_ref[...], preferred_element_type=F32)
              + gb * jnp.dot(b_ref[...], wpb_ref[...], preferred_element_type=F32))
    x = x_ref[...] + jnp.dot(merged.astype(BF16), wo_ref[...], preferred_element_type=F32)
    o_ref[...] = _rms(_macaron_half_step(x, g_ref[...], win_ref, wout_ref), fg_ref[...])


def _merge_ffn(x, a, b, gates, wpa, wpb, wo, g, w_in, w_out, final_g):
    n = x.shape[0]
    row = lambda w: pl.BlockSpec((ROW_TILE, w), lambda i: (i, 0))
    return pl.pallas_call(
        _merge_ffn_body,
        grid=(n // ROW_TILE,),
        in_specs=[row(D_MODEL), row(A_WIDTH), row(B_V), row(2 * D_MODEL),
                  _const_spec((A_WIDTH, D_MODEL)), _const_spec((B_V, D_MODEL)),
                  _const_spec((D_MODEL, D_MODEL)), _const_spec((1, D_MODEL)),
                  _const_spec((D_MODEL, 2 * D_FF)), _const_spec((D_FF, D_MODEL)),
                  _const_spec((1, D_MODEL))],
        out_specs=row(D_MODEL),
        out_shape=jax.ShapeDtypeStruct((n, D_MODEL), F32),
        compiler_params=_params(1),
        name="merge_ffn",
    )(x, a, b, gates, wpa, wpb, wo, g, w_in, w_out, final_g)


def _layer(x, lp, tiles, tiles_t, lam, out_scale, single_pass):
    b, s, _ = x.shape
    x0 = x.reshape(b * s, D_MODEL)
    x1 = _ffn(x0, lp["ffn1_norm"], lp["ffn1_w_in"], lp["ffn1_w_out"])
    a, q, k, v, gates = _proj(x1, lp["mix_norm"], lp["w_in"], lp["gate_bias"], lp["sgu_norm"],
                              lp["wcat"], lp["sbias"], lp["seg"], lp["qg"], lp["kg"])
    bo = lax.cond(single_pass,
                  functools.partial(_attn, out_scale=out_scale, single_pass=True),
                  functools.partial(_attn, out_scale=out_scale, single_pass=False),
                  lam, q.reshape(b, s, B_QK), k.reshape(b, s, B_QK), v.reshape(b, s, B_V),
                  tiles, tiles_t, lp["diff_subln"])
    y = _merge_ffn(x1, a, bo.reshape(b * s, B_V), gates, lp["w_proj_a"], lp["w_proj_b"], lp["w_out"],
                   lp["ffn2_norm"], lp["ffn2_w_in"], lp["ffn2_w_out"], lp["final_norm"])
    return y.reshape(b, s, D_MODEL)


def kernel(x_prompt, x_sample, rel_bias, ffn1_norm, ffn1_w_in, ffn1_w_out, mix_norm, w_in, gate_bias, sgu_norm, sgu_w, sgu_b, q_norm, k_norm, lambda_q1, lambda_k1, lambda_q2, lambda_k2, diff_subln, w_proj_a, w_proj_b, w_out, ffn2_norm, ffn2_w_in, ffn2_w_out, final_norm):
    depth = ffn1_norm.shape[0]
    y_prompt, y_sample = x_prompt, x_sample
    row = lambda p: p.reshape(1, -1).astype(F32)
    for l in range(depth):
        lambda_init = 0.8 - 0.6 * math.exp(-0.3 * l)
        sw = sgu_w[l].astype(BF16)
        lp = {
            "ffn1_norm": row(ffn1_norm[l]), "ffn1_w_in": ffn1_w_in[l].astype(BF16),
            "ffn1_w_out": ffn1_w_out[l].astype(BF16),
            "mix_norm": row(mix_norm[l]), "w_in": w_in[l].astype(BF16),
            "gate_bias": row(gate_bias[l]), "sgu_norm": row(sgu_norm[l]),
            "wcat": jnp.concatenate([sw[0::2], sw[1::2]], axis=-1),
            "sbias": jnp.repeat(sgu_b[l].T.astype(F32), A_GROUP_DIM, axis=1),
            "seg": jnp.kron(jnp.eye(V7X_MXU_DIM // HEAD_DIM, dtype=F32),
                            jnp.full((HEAD_DIM, HEAD_DIM), 1.0 / HEAD_DIM, F32)).astype(BF16),
            "qg": jnp.tile(row(q_norm[l]) * (HEAD_DIM ** -0.5 * LOG2E), (1, B_QK // HEAD_DIM)),
            "kg": jnp.tile(row(k_norm[l]), (1, B_QK // HEAD_DIM)),
            "diff_subln": row(diff_subln[l]),
            "w_proj_a": w_proj_a[l].astype(BF16), "w_proj_b": w_proj_b[l].astype(BF16),
            "w_out": w_out[l].astype(BF16),
            "ffn2_norm": row(ffn2_norm[l]), "ffn2_w_in": ffn2_w_in[l].astype(BF16),
            "ffn2_w_out": ffn2_w_out[l].astype(BF16), "final_norm": row(final_norm[l]),
        }
        shift, single_pass = _logit_shift(lp["qg"], lp["kg"], rel_bias)
        tiles, tiles_t, lam = _prep(rel_bias.astype(F32), shift, row(lambda_q1[l]), row(lambda_k1[l]),
                                    row(lambda_q2[l]), row(lambda_k2[l]), lambda_init)
        out_scale = 1.0 - lambda_init
        y_prompt = _layer(y_prompt, lp, tiles, tiles_t, lam, out_scale, single_pass)
        y_sample = _layer(y_sample, lp, tiles, tiles_t, lam, out_scale, single_pass)
    return (y_prompt, y_sample)
```

```python
import functools
import math

import jax
import jax.numpy as jnp
from jax import lax
from jax.experimental import pallas as pl
from jax.experimental.pallas import tpu as pltpu

D_MODEL = 1024
D_FF = 2816
CHUNK = 128
A_WIDTH = 512
A_GROUPS = 8
A_GROUP_DIM = A_WIDTH // A_GROUPS
B_HEADS = 4
HEAD_DIM = 64
HEAD_WIDTH = 2 * HEAD_DIM
B_QK = B_HEADS * HEAD_WIDTH
B_V = B_HEADS * HEAD_WIDTH
N_BUCKETS = 32
MAX_DISTANCE = 128
IN_COLS = 2 * A_WIDTH + 2 * B_QK + B_V + 2 * D_MODEL
EPS = 1e-6

V7X_LANES = 128
V7X_MXU_DIM = 256
V7X_VMEM_LIMIT_BYTES = 56 * 1024 * 1024

ROW_TILE = 512
ROW_SUB_TILES = 2
ATTN_TILE = 256
ATTN_KEY_STEP = 512
LOG2E = math.log2(math.e)
LOGIT_SPAN = 100.0
N_BIAS_TILES = 5

F32 = jnp.float32
BF16 = jnp.bfloat16


def _rms(x, g):
    return x * lax.rsqrt(jnp.mean(x * x, axis=-1, keepdims=True) + EPS) * g


def _const_spec(shape):
    return pl.BlockSpec(shape, lambda *_: (0,) * len(shape), pipeline_mode=pl.Buffered(1))


def _params(n_axes):
    return pltpu.CompilerParams(dimension_semantics=("parallel",) * n_axes,
                                vmem_limit_bytes=V7X_VMEM_LIMIT_BYTES)


def _macaron_half_step(x, g, win_ref, wout_ref):
    h = _rms(x, g).astype(BF16)
    gu = jnp.dot(h, win_ref[...], preferred_element_type=F32)
    gate = gu[:, :D_FF]
    up = gu[:, D_FF:]
    act = (gate * jax.nn.sigmoid(gate) * up).astype(BF16)
    return x + 0.5 * jnp.dot(act, wout_ref[...], preferred_element_type=F32)


def _sub_tiles(n_rows):
    step = n_rows // ROW_SUB_TILES
    return [slice(r * step, (r + 1) * step) for r in range(ROW_SUB_TILES)]


def _ffn_body(x_ref, g_ref, win_ref, wout_ref, o_ref):
    for rows in _sub_tiles(x_ref.shape[0]):
        o_ref[rows, :] = _macaron_half_step(x_ref[rows, :], g_ref[...], win_ref, wout_ref)


def _ffn(x, g, w_in, w_out):
    n = x.shape[0]
    row = pl.BlockSpec((ROW_TILE, D_MODEL), lambda i: (i, 0))
    return pl.pallas_call(
        _ffn_body,
        grid=(n // ROW_TILE,),
        in_specs=[row, _const_spec((1, D_MODEL)), _const_spec((D_MODEL, 2 * D_FF)),
                  _const_spec((D_FF, D_MODEL))],
        out_specs=row,
        out_shape=jax.ShapeDtypeStruct((n, D_MODEL), F32),
        compiler_params=_params(1),
        name="ffn",
    )(x, g, w_in, w_out)


def _proj_body(x_ref, *refs):
    params, outs = refs[:9], refs[9:]
    for rows in _sub_tiles(x_ref.shape[0]):
        _proj_rows(x_ref.at[rows], *params, *(o.at[rows] for o in outs))


def _proj_rows(x_ref, g_ref, win_ref, gbias_ref, sgug_ref, wcat_ref, sbias_ref, seg_ref,
               qg_ref, kg_ref, a_ref, q_ref, k_ref, v_ref, gate_ref):
    tm = x_ref.shape[0]
    h = _rms(x_ref[...], g_ref[...]).astype(BF16)
    proj = jnp.dot(h, win_ref[...], preferred_element_type=F32)
    c0 = 2 * A_WIDTH
    c1 = c0 + B_QK
    c2 = c1 + B_QK
    c3 = c2 + B_V

    uv = jax.nn.gelu(proj[:, :c0])
    u = uv[:, :A_WIDTH]
    vn = _rms(uv[:, A_WIDTH:], sgug_ref[...])
    lane = lax.broadcasted_iota(jnp.int32, (tm, A_WIDTH), 1)
    low = (lane & (V7X_LANES - 1)) < A_GROUP_DIM
    v_low = jnp.where(low, vn, 0.0).astype(BF16)
    v_high = jnp.where(low, 0.0, vn).astype(BF16)
    for c in range(0, tm // CHUNK, 2):
        pair = [slice((c + i) * CHUNK, (c + i + 1) * CHUNK) for i in range(2)]
        slabs = []
        for p in range(A_WIDTH // V7X_LANES):
            cols = slice(p * V7X_LANES, (p + 1) * V7X_LANES)
            rhs = jnp.concatenate(
                [jnp.concatenate([v_low[rows, cols], v_high[rows, cols]], axis=0)
                 for rows in pair], axis=1)
            slabs.append(jnp.dot(wcat_ref[p], rhs, preferred_element_type=F32))
        for i, rows in enumerate(pair):
            s = jnp.concatenate([slab[:, i * V7X_LANES:(i + 1) * V7X_LANES] for slab in slabs],
                                axis=1) + sbias_ref[...]
            a_ref[rows, :] = (u[rows, :] * s).astype(a_ref.dtype)

    def seg_norm(t, gain):
        sq = (t * t).astype(BF16)
        half = seg_ref.shape[0]
        ms = jnp.concatenate(
            [jnp.dot(sq[:, i * half:(i + 1) * half], seg_ref[...], preferred_element_type=F32)
             for i in range(B_QK // half)], axis=1)
        return (t * lax.rsqrt(ms + EPS) * gain).astype(BF16)

    q_ref[...] = seg_norm(proj[:, c0:c1], qg_ref[...])
    k_ref[...] = seg_norm(proj[:, c1:c2], kg_ref[...])
    v_ref[...] = proj[:, c2:c3].astype(BF16)
    gate_ref[...] = jax.nn.sigmoid(proj[:, c3:] + gbias_ref[...]).astype(gate_ref.dtype)


def _proj(x, g, w_in, gate_bias, sgu_g, wcat, sbias, seg, qg, kg):
    n = x.shape[0]
    row = lambda w: pl.BlockSpec((ROW_TILE, w), lambda i: (i, 0))
    out_w = (A_WIDTH, B_QK, B_QK, B_V, 2 * D_MODEL)
    return pl.pallas_call(
        _proj_body,
        grid=(n // ROW_TILE,),
        in_specs=[row(D_MODEL), _const_spec((1, D_MODEL)), _const_spec((D_MODEL, IN_COLS)),
                  _const_spec((1, 2 * D_MODEL)), _const_spec((1, A_WIDTH)),
                  _const_spec(wcat.shape), _const_spec(sbias.shape), _const_spec(seg.shape),
                  _const_spec((1, B_QK)), _const_spec((1, B_QK))],
        out_specs=[row(w) for w in out_w],
        out_shape=[jax.ShapeDtypeStruct((n, w), BF16) for w in out_w],
        compiler_params=_params(1),
        name="proj",
    )(x, g, w_in, gate_bias, sgu_g, wcat, sbias, seg, qg, kg)


def _bucket_thresholds():
    half = N_BUCKETS // 2
    max_exact = half // 2
    n_log = half - max_exact
    ratio = MAX_DISTANCE // max_exact
    out = []
    for j in range(1, n_log):
        target = max_exact ** n_log * ratio ** j
        n = max_exact
        while n ** n_log < target:
            n += 1
        out.append(n)
    return max_exact, out


def _prep_body(rb_ref, shift_ref, lq1_ref, lk1_ref, lq2_ref, lk2_ref, tiles_ref, tiles_t_ref,
               lam_ref, *, lambda_init):
    t = ATTN_TILE
    half = N_BUCKETS // 2
    max_exact, thresholds = _bucket_thresholds()
    row = lax.broadcasted_iota(jnp.int32, (t, t), 0)
    col = lax.broadcasted_iota(jnp.int32, (t, t), 1)
    shift = shift_ref[0, 0]
    for out_ref, query, key in ((tiles_ref, row, col), (tiles_t_ref, col, row)):
        for hd in range(B_HEADS):
            out_ref[hd, 0] = jnp.full((t, t), rb_ref[half - 1, hd] * LOG2E - shift, F32)
            out_ref[hd, N_BIAS_TILES - 1] = jnp.full(
                (t, t), rb_ref[N_BUCKETS - 1, hd] * LOG2E - shift, F32)
        for d in (-1, 0, 1):
            rel = key - query + d * t
            n = jnp.abs(rel)
            large = jnp.full((t, t), max_exact, jnp.int32)
            for th in thresholds:
                large = large + jnp.where(n >= th, 1, 0)
            bucket = jnp.where(rel > 0, half, 0) + jnp.where(n < max_exact, n, large)
            for hd in range(B_HEADS):
                tile = jnp.zeros((t, t), F32)
                for b in range(N_BUCKETS):
                    tile = jnp.where(bucket == b, rb_ref[b, hd] * LOG2E - shift, tile)
                out_ref[hd, d + N_BIAS_TILES // 2] = tile
    s1 = jnp.sum(lq1_ref[...] * lk1_ref[...], axis=-1, keepdims=True)
    s2 = jnp.sum(lq2_ref[...] * lk2_ref[...], axis=-1, keepdims=True)
    lam = jnp.exp(s1) - jnp.exp(s2) + lambda_init
    lam_ref[...] = jnp.broadcast_to(lam, lam_ref.shape)


def _prep(rel_bias, shift, lq1, lk1, lq2, lk2, lambda_init):
    t = ATTN_TILE
    vec = pl.BlockSpec((1, HEAD_DIM), lambda: (0, 0))
    return pl.pallas_call(
        functools.partial(_prep_body, lambda_init=lambda_init),
        in_specs=[pl.BlockSpec(memory_space=pltpu.SMEM), pl.BlockSpec(memory_space=pltpu.SMEM),
                  vec, vec, vec, vec],
        out_specs=[pl.BlockSpec((B_HEADS, N_BIAS_TILES, t, t), lambda: (0, 0, 0, 0)),
                   pl.BlockSpec((B_HEADS, N_BIAS_TILES, t, t), lambda: (0, 0, 0, 0)),
                   pl.BlockSpec((8, V7X_LANES), lambda: (0, 0))],
        out_shape=[jax.ShapeDtypeStruct((B_HEADS, N_BIAS_TILES, t, t), F32),
                   jax.ShapeDtypeStruct((B_HEADS, N_BIAS_TILES, t, t), F32),
                   jax.ShapeDtypeStruct((8, V7X_LANES), F32)],
        name="prep",
    )(rel_bias, shift, lq1, lk1, lq2, lk2)


def _q_stack(q_ref, c):
    t = ATTN_TILE
    lane = lax.broadcasted_iota(jnp.int32, (t, HEAD_WIDTH), 1)
    first = jnp.where(lane < HEAD_DIM, 1.0, 0.0).astype(BF16)
    q = q_ref[pl.ds(pl.multiple_of(c * t, t), t), :]
    return jnp.concatenate([q * first, q * (1.0 - first)], axis=0)


def _biased_logits(q2, c, j, k_ref, tiles_ref):
    t = ATTN_TILE
    ks = ATTN_KEY_STEP
    bias = jnp.concatenate(
        [tiles_ref[jnp.clip(jt - c, -2, 2) + N_BIAS_TILES // 2]
         for jt in range(j * ks // t, (j + 1) * ks // t)], axis=1)
    s = lax.dot_general(q2, k_ref[j * ks:(j + 1) * ks, :], (((1,), (1,)), ((), ())),
                        preferred_element_type=F32)
    return s + jnp.concatenate([bias, bias], axis=0)


def _store_output(res, c, lam_ref, subg_ref, o_ref, out_scale):
    t = ATTN_TILE
    out = res[:t] - lam_ref[0:1, 0:1] * res[t:]
    o_ref[pl.ds(pl.multiple_of(c * t, t), t), :] = (
        _rms(out, subg_ref[...]) * out_scale).astype(o_ref.dtype)


def _attn_exact_body(lam_ref, q_ref, k_ref, v_ref, tiles_ref, subg_ref, o_ref,
                     s_ref, m_ref, mnext_ref, l_ref, acc_ref, *, out_scale):
    t = ATTN_TILE
    ks = ATTN_KEY_STEP
    nk = k_ref.shape[0] // ks
    nq = q_ref.shape[0] // t
    reps = ks // V7X_LANES

    def logits_step(c, q2, j):
        s = _biased_logits(q2, c, j, k_ref, tiles_ref)
        s_ref[:, j * ks:(j + 1) * ks] = s
        blockmax = s[:, :V7X_LANES]
        for r in range(1, reps):
            blockmax = jnp.maximum(blockmax, s[:, r * V7X_LANES:(r + 1) * V7X_LANES])
        if j == 0:
            mnext_ref[...] = blockmax
        else:
            mnext_ref[...] = jnp.maximum(mnext_ref[...], blockmax)

    def finish_logits():
        m = jnp.max(mnext_ref[...], axis=1, keepdims=True)
        m_ref[...] = jnp.broadcast_to(m, m_ref.shape)

    def prob_step(j, s):
        p = jnp.exp2(s - jnp.concatenate([m_ref[...]] * reps, axis=1))
        psum = p[:, :V7X_LANES]
        for r in range(1, reps):
            psum = psum + p[:, r * V7X_LANES:(r + 1) * V7X_LANES]
        pv = jnp.dot(p.astype(BF16), v_ref[j * ks:(j + 1) * ks, :], preferred_element_type=F32)
        if j == 0:
            l_ref[...] = psum
            acc_ref[...] = pv
        else:
            l_ref[...] += psum
            acc_ref[...] += pv

    def finish_block(c):
        l = jnp.sum(l_ref[...], axis=1, keepdims=True)
        _store_output(acc_ref[...] / l, c, lam_ref, subg_ref, o_ref, out_scale)

    q2 = _q_stack(q_ref, 0)
    for j in range(nk):
        logits_step(0, q2, j)
    finish_logits()

    def body(c, carry):
        q2n = _q_stack(q_ref, c + 1)
        for j in range(nk):
            s = s_ref[:, j * ks:(j + 1) * ks]
            logits_step(c + 1, q2n, j)
            prob_step(j, s)
        finish_block(c)
        finish_logits()
        return carry

    lax.fori_loop(0, nq - 1, body, 0)

    for j in range(nk):
        prob_step(j, s_ref[:, j * ks:(j + 1) * ks])
    finish_block(nq - 1)


def _attn_fast_body(lam_ref, q_ref, k_ref, v_ref, tiles_t_ref, subg_ref, o_ref, pt_ref, vt_ref,
                    l_ref, *, out_scale):
    t = ATTN_TILE
    ks = ATTN_KEY_STEP
    seq = k_ref.shape[0]
    nk = seq // ks
    nq = q_ref.shape[0] // t

    vt_ref[...] = v_ref[...].astype(F32).T.astype(BF16)

    def prob_block(c):
        q2 = _q_stack(q_ref, c)
        total = None
        for j in range(nk):
            keys = slice(j * ks, (j + 1) * ks)
            bias = jnp.concatenate(
                [tiles_t_ref[jnp.clip(jt - c, -2, 2) + N_BIAS_TILES // 2]
                 for jt in range(j * ks // t, (j + 1) * ks // t)], axis=0)
            s = lax.dot_general(k_ref[keys, :], q2, (((1,), (1,)), ((), ())),
                                preferred_element_type=F32)
            p = jnp.exp2(s + jnp.concatenate([bias, bias], axis=1))
            part = jnp.sum(p, axis=0, keepdims=True)
            total = part if total is None else total + part
            pt_ref[keys, :] = p.astype(BF16)
        l_ref[...] = jnp.broadcast_to(total, l_ref.shape)

    def output_block(c):
        res = jnp.dot(vt_ref[...], pt_ref[...], preferred_element_type=F32)
        res = res / l_ref[0:1, :]
        out = res[:, :t] - lam_ref[0:1, 0:1] * res[:, t:]
        out = out * lax.rsqrt(jnp.mean(out * out, axis=0, keepdims=True) + EPS)
        o_ref[pl.ds(pl.multiple_of(c * t, t), t), :] = (
            out.T * subg_ref[...] * out_scale).astype(o_ref.dtype)

    prob_block(0)

    def body(c, carry):
        output_block(c - 1)
        prob_block(c)
        return carry

    lax.fori_loop(1, nq, body, 0, unroll=4)
    output_block(nq - 1)


def _attn(lam, q, k, v, tiles, tiles_t, subg, *, out_scale, single_pass):
    b, s, _ = q.shape
    t = ATTN_TILE
    seq = pl.BlockSpec((None, s, HEAD_WIDTH), lambda bi, hi: (bi, 0, hi))
    stack = pltpu.VMEM((2 * t, V7X_LANES), F32)
    if single_pass:
        body, name, bias_tiles = _attn_fast_body, "attn_fast", tiles_t
        scratch = [pltpu.VMEM((s, 2 * t), BF16), pltpu.VMEM((HEAD_WIDTH, s), BF16),
                   pltpu.VMEM((8, 2 * t), F32)]
    else:
        body, name, bias_tiles = _attn_exact_body, "attn_exact", tiles
        scratch = [pltpu.VMEM((2 * t, s), F32), stack, stack, stack, stack]
    return pl.pallas_call(
        functools.partial(body, out_scale=out_scale),
        grid=(b, B_HEADS),
        in_specs=[pl.BlockSpec((8, V7X_LANES), lambda bi, hi: (0, 0)),
                  seq, seq, seq,
                  pl.BlockSpec((None, N_BIAS_TILES, t, t), lambda bi, hi: (hi, 0, 0, 0)),
                  pl.BlockSpec((1, HEAD_WIDTH), lambda bi, hi: (0, 0))],
        out_specs=seq,
        out_shape=jax.ShapeDtypeStruct((b, s, B_V), BF16),
        scratch_shapes=scratch,
        compiler_params=_params(2),
        name=name,
    )(lam, q, k, v, bias_tiles, subg)


def _logit_shift(q_gain, k_gain, rel_bias):
    qk_bound = 1.02 * HEAD_DIM * jnp.max(jnp.abs(q_gain)) * jnp.max(jnp.abs(k_gain))
    bias = rel_bias.astype(F32) * LOG2E
    shift = qk_bound + jnp.max(bias)
    span = 2.0 * qk_bound + jnp.max(bias) - jnp.min(bias)
    return shift.reshape(1, 1), span <= LOGIT_SPAN


def _merge_ffn_body(x_ref, a_ref, b_ref, gate_ref, wpa_ref, wpb_ref, wo_ref, g_ref, win_ref,
                    wout_ref, fg_ref, o_ref):
    ga = gate_ref[:, :D_MODEL].astype(F32)
    gb = gate_ref[:, D_MODEL:].astype(F32)
    merged = (ga * jnp.dot(a_ref[...], wpa_ref[...], preferred_element_type=F32)
              + gb * jnp.dot(b_ref[...], wpb_ref[...], preferred_element_type=F32))
    x = x_ref[...] + jnp.dot(merged.astype(BF16), wo_ref[...], preferred_element_type=F32)
    o_ref[...] = _rms(_macaron_half_step(x, g_ref[...], win_ref, wout_ref), fg_ref[...])


def _merge_ffn(x, a, b, gates, wpa, wpb, wo, g, w_in, w_out, final_g):
    n = x.shape[0]
    row = lambda w: pl.BlockSpec((ROW_TILE, w), lambda i: (i, 0))
    return pl.pallas_call(
        _merge_ffn_body,
        grid=(n // ROW_TILE,),
        in_specs=[row(D_MODEL), row(A_WIDTH), row(B_V), row(2 * D_MODEL),
                  _const_spec((A_WIDTH, D_MODEL)), _const_spec((B_V, D_MODEL)),
                  _const_spec((D_MODEL, D_MODEL)), _const_spec((1, D_MODEL)),
                  _const_spec((D_MODEL, 2 * D_FF)), _const_spec((D_FF, D_MODEL)),
                  _const_spec((1, D_MODEL))],
        out_specs=row(D_MODEL),
        out_shape=jax.ShapeDtypeStruct((n, D_MODEL), F32),
        compiler_params=_params(1),
        name="merge_ffn",
    )(x, a, b, gates, wpa, wpb, wo, g, w_in, w_out, final_g)


def _layer(x, lp, tiles, tiles_t, lam, out_scale, single_pass):
    b, s, _ = x.shape
    x0 = x.reshape(b * s, D_MODEL)
    x1 = _ffn(x0, lp["ffn1_norm"], lp["ffn1_w_in"], lp["ffn1_w_out"])
    a, q, k, v, gates = _proj(x1, lp["mix_norm"], lp["w_in"], lp["gate_bias"], lp["sgu_norm"],
                              lp["wcat"], lp["sbias"], lp["seg"], lp["qg"], lp["kg"])
    bo = lax.cond(single_pass,
                  functools.partial(_attn, out_scale=out_scale, single_pass=True),
                  functools.partial(_attn, out_scale=out_scale, single_pass=False),
                  lam, q.reshape(b, s, B_QK), k.reshape(b, s, B_QK), v.reshape(b, s, B_V),
                  tiles, tiles_t, lp["diff_subln"])
    y = _merge_ffn(x1, a, bo.reshape(b * s, B_V), gates, lp["w_proj_a"], lp["w_proj_b"], lp["w_out"],
                   lp["ffn2_norm"], lp["ffn2_w_in"], lp["ffn2_w_out"], lp["final_norm"])
    return y.reshape(b, s, D_MODEL)


def kernel(x_prompt, x_sample, rel_bias, ffn1_norm, ffn1_w_in, ffn1_w_out, mix_norm, w_in, gate_bias, sgu_norm, sgu_w, sgu_b, q_norm, k_norm, lambda_q1, lambda_k1, lambda_q2, lambda_k2, diff_subln, w_proj_a, w_proj_b, w_out, ffn2_norm, ffn2_w_in, ffn2_w_out, final_norm):
    depth = ffn1_norm.shape[0]
    y_prompt, y_sample = x_prompt, x_sample
    row = lambda p: p.reshape(1, -1).astype(F32)
    for l in range(depth):
        lambda_init = 0.8 - 0.6 * math.exp(-0.3 * l)
        sw = sgu_w[l].astype(BF16)
        lp = {
            "ffn1_norm": row(ffn1_norm[l]), "ffn1_w_in": ffn1_w_in[l].astype(BF16),
            "ffn1_w_out": ffn1_w_out[l].astype(BF16),
            "mix_norm": row(mix_norm[l]), "w_in": w_in[l].astype(BF16),
            "gate_bias": row(gate_bias[l]), "sgu_norm": row(sgu_norm[l]),
            "wcat": jnp.concatenate([sw[0::2], sw[1::2]], axis=-1),
            "sbias": jnp.repeat(sgu_b[l].T.astype(F32), A_GROUP_DIM, axis=1),
            "seg": jnp.kron(jnp.eye(V7X_MXU_DIM // HEAD_DIM, dtype=F32),
                            jnp.full((HEAD_DIM, HEAD_DIM), 1.0 / HEAD_DIM, F32)).astype(BF16),
            "qg": jnp.tile(row(q_norm[l]) * (HEAD_DIM ** -0.5 * LOG2E), (1, B_QK // HEAD_DIM)),
            "kg": jnp.tile(row(k_norm[l]), (1, B_QK // HEAD_DIM)),
            "diff_subln": row(diff_subln[l]),
            "w_proj_a": w_proj_a[l].astype(BF16), "w_proj_b": w_proj_b[l].astype(BF16),
            "w_out": w_out[l].astype(BF16),
            "ffn2_norm": row(ffn2_norm[l]), "ffn2_w_in": ffn2_w_in[l].astype(BF16),
            "ffn2_w_out": ffn2_w_out[l].astype(BF16), "final_norm": row(final_norm[l]),
        }
        shift, single_pass = _logit_shift(lp["qg"], lp["kg"], rel_bias)
        tiles, tiles_t, lam = _prep(rel_bias.astype(F32), shift, row(lambda_q1[l]), row(lambda_k1[l]),
                                    row(lambda_q2[l]), row(lambda_k2[l]), lambda_init)
        out_scale = 1.0 - lambda_init
        y_prompt = _layer(y_prompt, lp, tiles, tiles_t, lam, out_scale, single_pass)
        y_sample = _layer(y_sample, lp, tiles, tiles_t, lam, out_scale, single_pass)
    return (y_prompt, y_sample)
```

```python
import functools
import math

import jax
import jax.numpy as jnp
from jax import lax
from jax.experimental import pallas as pl
from jax.experimental.pallas import tpu as pltpu

D_MODEL = 1024
D_FF = 2816
CHUNK = 128
A_WIDTH = 512
A_GROUPS = 8
A_GROUP_DIM = A_WIDTH // A_GROUPS
B_HEADS = 4
HEAD_DIM = 64
HEAD_WIDTH = 2 * HEAD_DIM
B_QK = B_HEADS * HEAD_WIDTH
B_V = B_HEADS * HEAD_WIDTH
N_BUCKETS = 32
MAX_DISTANCE = 128
IN_COLS = 2 * A_WIDTH + 2 * B_QK + B_V + 2 * D_MODEL
EPS = 1e-6

V7X_LANES = 128
V7X_MXU_DIM = 256
V7X_VMEM_LIMIT_BYTES = 56 * 1024 * 1024

ROW_TILE = 512
WIDE_ROW_TILE = 1024
SUB_TILE_ROWS = 256
ATTN_TILE = 256
ATTN_KEY_STEP = 512
LOG2E = math.log2(math.e)
LOGIT_SPAN = 100.0
N_BIAS_TILES = 5

F32 = jnp.float32
BF16 = jnp.bfloat16


def _rms(x, g):
    return x * lax.rsqrt(jnp.mean(x * x, axis=-1, keepdims=True) + EPS) * g


def _const_spec(shape):
    return pl.BlockSpec(shape, lambda *_: (0,) * len(shape), pipeline_mode=pl.Buffered(1))


def _params(n_axes):
    return pltpu.CompilerParams(dimension_semantics=("parallel",) * n_axes,
                                vmem_limit_bytes=V7X_VMEM_LIMIT_BYTES)


def _macaron_half_step(x, g, win_ref, wout_ref):
    h = _rms(x, g).astype(BF16)
    gu = jnp.dot(h, win_ref[...], preferred_element_type=F32)
    gate = gu[:, :D_FF]
    up = gu[:, D_FF:]
    act = (gate * jax.nn.sigmoid(gate) * up).astype(BF16)
    return x + 0.5 * jnp.dot(act, wout_ref[...], preferred_element_type=F32)


def _sub_tiles(n_rows):
    return [slice(r, r + SUB_TILE_ROWS) for r in range(0, n_rows, SUB_TILE_ROWS)]


def _ffn_body(x_ref, g_ref, win_ref, wout_ref, o_ref):
    for rows in _sub_tiles(x_ref.shape[0]):
        o_ref[rows, :] = _macaron_half_step(x_ref[rows, :], g_ref[...], win_ref, wout_ref)


def _ffn(x, g, w_in, w_out):
    n = x.shape[0]
    row = pl.BlockSpec((WIDE_ROW_TILE, D_MODEL), lambda i: (i, 0))
    return pl.pallas_call(
        _ffn_body,
        grid=(n // WIDE_ROW_TILE,),
        in_specs=[row, _const_spec((1, D_MODEL)), _const_spec((D_MODEL, 2 * D_FF)),
                  _const_spec((D_FF, D_MODEL))],
        out_specs=row,
        out_shape=jax.ShapeDtypeStruct((n, D_MODEL), F32),
        compiler_params=_params(1),
        name="ffn",
    )(x, g, w_in, w_out)


def _proj_body(x_ref, *refs):
    params, outs = refs[:9], refs[9:]
    for rows in _sub_tiles(x_ref.shape[0]):
        _proj_rows(x_ref.at[rows], *params, *(o.at[rows] for o in outs))


def _proj_rows(x_ref, g_ref, win_ref, gbias_ref, sgug_ref, wcat_ref, sbias_ref, seg_ref,
               qg_ref, kg_ref, a_ref, q_ref, k_ref, v_ref, gate_ref):
    tm = x_ref.shape[0]
    h = _rms(x_ref[...], g_ref[...]).astype(BF16)
    proj = jnp.dot(h, win_ref[...], preferred_element_type=F32)
    c0 = 2 * A_WIDTH
    c1 = c0 + B_QK
    c2 = c1 + B_QK
    c3 = c2 + B_V

    uv = jax.nn.gelu(proj[:, :c0])
    u = uv[:, :A_WIDTH]
    vn = _rms(uv[:, A_WIDTH:], sgug_ref[...])
    lane = lax.broadcasted_iota(jnp.int32, (tm, A_WIDTH), 1)
    low = (lane & (V7X_LANES - 1)) < A_GROUP_DIM
    v_low = jnp.where(low, vn, 0.0).astype(BF16)
    v_high = jnp.where(low, 0.0, vn).astype(BF16)
    for c in range(0, tm // CHUNK, 2):
        pair = [slice((c + i) * CHUNK, (c + i + 1) * CHUNK) for i in range(2)]
        slabs = []
        for p in range(A_WIDTH // V7X_LANES):
            cols = slice(p * V7X_LANES, (p + 1) * V7X_LANES)
            rhs = jnp.concatenate(
                [jnp.concatenate([v_low[rows, cols], v_high[rows, cols]], axis=0)
                 for rows in pair], axis=1)
            slabs.append(jnp.dot(wcat_ref[p], rhs, preferred_element_type=F32))
        for i, rows in enumerate(pair):
            s = jnp.concatenate([slab[:, i * V7X_LANES:(i + 1) * V7X_LANES] for slab in slabs],
                                axis=1) + sbias_ref[...]
            a_ref[rows, :] = (u[rows, :] * s).astype(a_ref.dtype)

    def seg_norm(t, gain):
        sq = (t * t).astype(BF16)
        half = seg_ref.shape[0]
        ms = jnp.concatenate(
            [jnp.dot(sq[:, i * half:(i + 1) * half], seg_ref[...], preferred_element_type=F32)
             for i in range(B_QK // half)], axis=1)
        return (t * lax.rsqrt(ms + EPS) * gain).astype(BF16)

    q_ref[...] = seg_norm(proj[:, c0:c1], qg_ref[...])
    k_ref[...] = seg_norm(proj[:, c1:c2], kg_ref[...])
    v_ref[...] = proj[:, c2:c3].astype(BF16)
    gate_ref[...] = jax.nn.sigmoid(proj[:, c3:] + gbias_ref[...]).astype(gate_ref.dtype)


def _proj(x, g, w_in, gate_bias, sgu_g, wcat, sbias, seg, qg, kg):
    n = x.shape[0]
    row = lambda w: pl.BlockSpec((WIDE_ROW_TILE, w), lambda i: (i, 0))
    out_w = (A_WIDTH, B_QK, B_QK, B_V, 2 * D_MODEL)
    return pl.pallas_call(
        _proj_body,
        grid=(n // WIDE_ROW_TILE,),
        in_specs=[row(D_MODEL), _const_spec((1, D_MODEL)), _const_spec((D_MODEL, IN_COLS)),
                  _const_spec((1, 2 * D_MODEL)), _const_spec((1, A_WIDTH)),
                  _const_spec(wcat.shape), _const_spec(sbias.shape), _const_spec(seg.shape),
                  _const_spec((1, B_QK)), _const_spec((1, B_QK))],
        out_specs=[row(w) for w in out_w],
        out_shape=[jax.ShapeDtypeStruct((n, w), BF16) for w in out_w],
        compiler_params=_params(1),
        name="proj",
    )(x, g, w_in, gate_bias, sgu_g, wcat, sbias, seg, qg, kg)


def _bucket_thresholds():
    half = N_BUCKETS // 2
    max_exact = half // 2
    n_log = half - max_exact
    ratio = MAX_DISTANCE // max_exact
    out = []
    for j in range(1, n_log):
        target = max_exact ** n_log * ratio ** j
        n = max_exact
        while n ** n_log < target:
            n += 1
        out.append(n)
    return max_exact, out


def _prep_body(rb_ref, shift_ref, lq1_ref, lk1_ref, lq2_ref, lk2_ref, tiles_ref, tiles_t_ref,
               lam_ref, *, lambda_init):
    t = ATTN_TILE
    half = N_BUCKETS // 2
    max_exact, thresholds = _bucket_thresholds()
    row = lax.broadcasted_iota(jnp.int32, (t, t), 0)
    col = lax.broadcasted_iota(jnp.int32, (t, t), 1)
    shift = shift_ref[0, 0]
    for out_ref, query, key in ((tiles_ref, row, col), (tiles_t_ref, col, row)):
        for hd in range(B_HEADS):
            out_ref[hd, 0] = jnp.full((t, t), rb_ref[half - 1, hd] * LOG2E - shift, F32)
            out_ref[hd, N_BIAS_TILES - 1] = jnp.full(
                (t, t), rb_ref[N_BUCKETS - 1, hd] * LOG2E - shift, F32)
        for d in (-1, 0, 1):
            rel = key - query + d * t
            n = jnp.abs(rel)
            large = jnp.full((t, t), max_exact, jnp.int32)
            for th in thresholds:
                large = large + jnp.where(n >= th, 1, 0)
            bucket = jnp.where(rel > 0, half, 0) + jnp.where(n < max_exact, n, large)
            for hd in range(B_HEADS):
                tile = jnp.zeros((t, t), F32)
                for b in range(N_BUCKETS):
                    tile = jnp.where(bucket == b, rb_ref[b, hd] * LOG2E - shift, tile)
                out_ref[hd, d + N_BIAS_TILES // 2] = tile
    s1 = jnp.sum(lq1_ref[...] * lk1_ref[...], axis=-1, keepdims=True)
    s2 = jnp.sum(lq2_ref[...] * lk2_ref[...], axis=-1, keepdims=True)
    lam = jnp.exp(s1) - jnp.exp(s2) + lambda_init
    lam_ref[...] = jnp.broadcast_to(lam, lam_ref.shape)


def _prep(rel_bias, shift, lq1, lk1, lq2, lk2, lambda_init):
    t = ATTN_TILE
    vec = pl.BlockSpec((1, HEAD_DIM), lambda: (0, 0))
    return pl.pallas_call(
        functools.partial(_prep_body, lambda_init=lambda_init),
        in_specs=[pl.BlockSpec(memory_space=pltpu.SMEM), pl.BlockSpec(memory_space=pltpu.SMEM),
                  vec, vec, vec, vec],
        out_specs=[pl.BlockSpec((B_HEADS, N_BIAS_TILES, t, t), lambda: (0, 0, 0, 0)),
                   pl.BlockSpec((B_HEADS, N_BIAS_TILES, t, t), lambda: (0, 0, 0, 0)),
                   pl.BlockSpec((8, V7X_LANES), lambda: (0, 0))],
        out_shape=[jax.ShapeDtypeStruct((B_HEADS, N_BIAS_TILES, t, t), F32),
                   jax.ShapeDtypeStruct((B_HEADS, N_BIAS_TILES, t, t), F32),
                   jax.ShapeDtypeStruct((8, V7X_LANES), F32)],
        name="prep",
    )(rel_bias, shift, lq1, lk1, lq2, lk2)


def _q_stack(q_ref, c):
    t = ATTN_TILE
    lane = lax.broadcasted_iota(jnp.int32, (t, HEAD_WIDTH), 1)
    first = jnp.where(lane < HEAD_DIM, 1.0, 0.0).astype(BF16)
    q = q_ref[pl.ds(pl.multiple_of(c * t, t), t), :]
    return jnp.concatenate([q * first, q * (1.0 - first)], axis=0)


def _biased_logits(q2, c, j, k_ref, tiles_ref):
    t = ATTN_TILE
    ks = ATTN_KEY_STEP
    bias = jnp.concatenate(
        [tiles_ref[jnp.clip(jt - c, -2, 2) + N_BIAS_TILES // 2]
         for jt in range(j * ks // t, (j + 1) * ks // t)], axis=1)
    s = lax.dot_general(q2, k_ref[j * ks:(j + 1) * ks, :], (((1,), (1,)), ((), ())),
                        preferred_element_type=F32)
    return s + jnp.concatenate([bias, bias], axis=0)


def _store_output(res, c, lam_ref, subg_ref, o_ref, out_scale):
    t = ATTN_TILE
    out = res[:t] - lam_ref[0:1, 0:1] * res[t:]
    o_ref[pl.ds(pl.multiple_of(c * t, t), t), :] = (
        _rms(out, subg_ref[...]) * out_scale).astype(o_ref.dtype)


def _attn_exact_body(lam_ref, q_ref, k_ref, v_ref, tiles_ref, subg_ref, o_ref,
                     s_ref, m_ref, mnext_ref, l_ref, acc_ref, *, out_scale):
    t = ATTN_TILE
    ks = ATTN_KEY_STEP
    nk = k_ref.shape[0] // ks
    nq = q_ref.shape[0] // t
    reps = ks // V7X_LANES

    def logits_step(c, q2, j):
        s = _biased_logits(q2, c, j, k_ref, tiles_ref)
        s_ref[:, j * ks:(j + 1) * ks] = s
        blockmax = s[:, :V7X_LANES]
        for r in range(1, reps):
            blockmax = jnp.maximum(blockmax, s[:, r * V7X_LANES:(r + 1) * V7X_LANES])
        if j == 0:
            mnext_ref[...] = blockmax
        else:
            mnext_ref[...] = jnp.maximum(mnext_ref[...], blockmax)

    def finish_logits():
        m = jnp.max(mnext_ref[...], axis=1, keepdims=True)
        m_ref[...] = jnp.broadcast_to(m, m_ref.shape)

    def prob_step(j, s):
        p = jnp.exp2(s - jnp.concatenate([m_ref[...]] * reps, axis=1))
        psum = p[:, :V7X_LANES]
        for r in range(1, reps):
            psum = psum + p[:, r * V7X_LANES:(r + 1) * V7X_LANES]
        pv = jnp.dot(p.astype(BF16), v_ref[j * ks:(j + 1) * ks, :], preferred_element_type=F32)
        if j == 0:
            l_ref[...] = psum
            acc_ref[...] = pv
        else:
            l_ref[...] += psum
            acc_ref[...] += pv

    def finish_block(c):
        l = jnp.sum(l_ref[...], axis=1, keepdims=True)
        _store_output(acc_ref[...] / l, c, lam_ref, subg_ref, o_ref, out_scale)

    q2 = _q_stack(q_ref, 0)
    for j in range(nk):
        logits_step(0, q2, j)
    finish_logits()

    def body(c, carry):
        q2n = _q_stack(q_ref, c + 1)
        for j in range(nk):
            s = s_ref[:, j * ks:(j + 1) * ks]
            logits_step(c + 1, q2n, j)
            prob_step(j, s)
        finish_block(c)
        finish_logits()
        return carry

    lax.fori_loop(0, nq - 1, body, 0)

    for j in range(nk):
        prob_step(j, s_ref[:, j * ks:(j + 1) * ks])
    finish_block(nq - 1)


def _attn_fast_body(lam_ref, q_ref, k_ref, v_ref, tiles_t_ref, subg_ref, o_ref, pt_ref, vt_ref,
                    l_ref, *, out_scale):
    t = ATTN_TILE
    ks = ATTN_KEY_STEP
    seq = k_ref.shape[0]
    nk = seq // ks
    nq = q_ref.shape[0] // t

    vt_ref[...] = v_ref[...].astype(F32).T.astype(BF16)

    def prob_block(c):
        q2 = _q_stack(q_ref, c)
        total = None
        for j in range(nk):
            keys = slice(j * ks, (j + 1) * ks)
            bias = jnp.concatenate(
                [tiles_t_ref[jnp.clip(jt - c, -2, 2) + N_BIAS_TILES // 2]
                 for jt in range(j * ks // t, (j + 1) * ks // t)], axis=0)
            s = lax.dot_general(k_ref[keys, :], q2, (((1,), (1,)), ((), ())),
                                preferred_element_type=F32)
            p = jnp.exp2(s + jnp.concatenate([bias, bias], axis=1))
            part = jnp.sum(p, axis=0, keepdims=True)
            total = part if total is None else total + part
            pt_ref[keys, :] = p.astype(BF16)
        l_ref[...] = jnp.broadcast_to(total, l_ref.shape)

    def output_block(c):
        res = jnp.dot(vt_ref[...], pt_ref[...], preferred_element_type=F32)
        res = res / l_ref[0:1, :]
        out = res[:, :t] - lam_ref[0:1, 0:1] * res[:, t:]
        out = out * lax.rsqrt(jnp.mean(out * out, axis=0, keepdims=True) + EPS)
        o_ref[pl.ds(pl.multiple_of(c * t, t), t), :] = (
            out.T * subg_ref[...] * out_scale).astype(o_ref.dtype)

    prob_block(0)

    def body(c, carry):
        output_block(c - 1)
        prob_block(c)
        return carry

    lax.fori_loop(1, nq, body, 0, unroll=4)
    output_block(nq - 1)


def _attn(lam, q, k, v, tiles, tiles_t, subg, *, out_scale, single_pass):
    b, s, _ = q.shape
    t = ATTN_TILE
    seq = pl.BlockSpec((None, s, HEAD_WIDTH), lambda bi, hi: (bi, 0, hi))
    stack = pltpu.VMEM((2 * t, V7X_LANES), F32)
    if single_pass:
        body, name, bias_tiles = _attn_fast_body, "attn_fast", tiles_t
        scratch = [pltpu.VMEM((s, 2 * t), BF16), pltpu.VMEM((HEAD_WIDTH, s), BF16),
                   pltpu.VMEM((8, 2 * t), F32)]
    else:
        body, name, bias_tiles = _attn_exact_body, "attn_exact", tiles
        scratch = [pltpu.VMEM((2 * t, s), F32), stack, stack, stack, stack]
    return pl.pallas_call(
        functools.partial(body, out_scale=out_scale),
        grid=(b, B_HEADS),
        in_specs=[pl.BlockSpec((8, V7X_LANES), lambda bi, hi: (0, 0)),
                  seq, seq, seq,
                  pl.BlockSpec((None, N_BIAS_TILES, t, t), lambda bi, hi: (hi, 0, 0, 0)),
                  pl.BlockSpec((1, HEAD_WIDTH), lambda bi, hi: (0, 0))],
        out_specs=seq,
        out_shape=jax.ShapeDtypeStruct((b, s, B_V), BF16),
        scratch_shapes=scratch,
        compiler_params=_params(2),
        name=name,
    )(lam, q, k, v, bias_tiles, subg)


def _logit_shift(q_gain, k_gain, rel_bias):
    qk_bound = 1.02 * HEAD_DIM * jnp.max(jnp.abs(q_gain)) * jnp.max(jnp.abs(k_gain))
    bias = rel_bias.astype(F32) * LOG2E
    shift = qk_bound + jnp.max(bias)
    span = 2.0 * qk_bound + jnp.max(bias) - jnp.min(bias)
    return shift.reshape(1, 1), span <= LOGIT_SPAN


def _merge_ffn_body(x_ref, a_ref, b_ref, gate_ref, wpa_ref, wpb_ref, wo_ref, g_ref, win_ref,
                    wout_ref, fg_ref, o_ref):
    ga = gate_ref[:, :D_MODEL].astype(F32)
    gb = gate_ref[:, D_MODEL:].astype(F32)
    merged = (ga * jnp.dot(a_ref[...], wpa_ref[...], preferred_element_type=F32)
              + gb * jnp.dot(b_ref[...], wpb_ref[...], preferred_element_type=F32))
    x = x_ref[...] + jnp.dot(merged.astype(BF16), wo_ref[...], preferred_element_type=F32)
    o_ref[...] = _rms(_macaron_half_step(x, g_ref[...], win_ref, wout_ref), fg_ref[...])


def _merge_ffn(x, a, b, gates, wpa, wpb, wo, g, w_in, w_out, final_g):
    n = x.shape[0]
    row = lambda w: pl.BlockSpec((ROW_TILE, w), lambda i: (i, 0))
    return pl.pallas_call(
        _merge_ffn_body,
        grid=(n // ROW_TILE,),
        in_specs=[row(D_MODEL), row(A_WIDTH), row(B_V), row(2 * D_MODEL),
                  _const_spec((A_WIDTH, D_MODEL)), _const_spec((B_V, D_MODEL)),
                  _const_spec((D_MODEL, D_MODEL)), _const_spec((1, D_MODEL)),
                  _const_spec((D_MODEL, 2 * D_FF)), _const_spec((D_FF, D_MODEL)),
                  _const_spec((1, D_MODEL))],
        out_specs=row(D_MODEL),
        out_shape=jax.ShapeDtypeStruct((n, D_MODEL), F32),
        compiler_params=_params(1),
        name="merge_ffn",
    )(x, a, b, gates, wpa, wpb, wo, g, w_in, w_out, final_g)


def _layer(x, lp, tiles, tiles_t, lam, out_scale, single_pass):
    b, s, _ = x.shape
    x0 = x.reshape(b * s, D_MODEL)
    x1 = _ffn(x0, lp["ffn1_norm"], lp["ffn1_w_in"], lp["ffn1_w_out"])
    a, q, k, v, gates = _proj(x1, lp["mix_norm"], lp["w_in"], lp["gate_bias"], lp["sgu_norm"],
                              lp["wcat"], lp["sbias"], lp["seg"], lp["qg"], lp["kg"])
    bo = lax.cond(single_pass,
                  functools.partial(_attn, out_scale=out_scale, single_pass=True),
                  functools.partial(_attn, out_scale=out_scale, single_pass=False),
                  lam, q.reshape(b, s, B_QK), k.reshape(b, s, B_QK), v.reshape(b, s, B_V),
                  tiles, tiles_t, lp["diff_subln"])
    y = _merge_ffn(x1, a, bo.reshape(b * s, B_V), gates, lp["w_proj_a"], lp["w_proj_b"], lp["w_out"],
                   lp["ffn2_norm"], lp["ffn2_w_in"], lp["ffn2_w_out"], lp["final_norm"])
    return y.reshape(b, s, D_MODEL)


def kernel(x_prompt, x_sample, rel_bias, ffn1_norm, ffn1_w_in, ffn1_w_out, mix_norm, w_in, gate_bias, sgu_norm, sgu_w, sgu_b, q_norm, k_norm, lambda_q1, lambda_k1, lambda_q2, lambda_k2, diff_subln, w_proj_a, w_proj_b, w_out, ffn2_norm, ffn2_w_in, ffn2_w_out, final_norm):
    depth = ffn1_norm.shape[0]
    y_prompt, y_sample = x_prompt, x_sample
    row = lambda p: p.reshape(1, -1).astype(F32)
    for l in range(depth):
        lambda_init = 0.8 - 0.6 * math.exp(-0.3 * l)
        sw = sgu_w[l].astype(BF16)
        lp = {
            "ffn1_norm": row(ffn1_norm[l]), "ffn1_w_in": ffn1_w_in[l].astype(BF16),
            "ffn1_w_out": ffn1_w_out[l].astype(BF16),
            "mix_norm": row(mix_norm[l]), "w_in": w_in[l].astype(BF16),
            "gate_bias": row(gate_bias[l]), "sgu_norm": row(sgu_norm[l]),
            "wcat": jnp.concatenate([sw[0::2], sw[1::2]], axis=-1),
            "sbias": jnp.repeat(sgu_b[l].T.astype(F32), A_GROUP_DIM, axis=1),
            "seg": jnp.kron(jnp.eye(V7X_MXU_DIM // HEAD_DIM, dtype=F32),
                            jnp.full((HEAD_DIM, HEAD_DIM), 1.0 / HEAD_DIM, F32)).astype(BF16),
            "qg": jnp.tile(row(q_norm[l]) * (HEAD_DIM ** -0.5 * LOG2E), (1, B_QK // HEAD_DIM)),
            "kg": jnp.tile(row(k_norm[l]), (1, B_QK // HEAD_DIM)),
            "diff_subln": row(diff_subln[l]),
            "w_proj_a": w_proj_a[l].astype(BF16), "w_proj_b": w_proj_b[l].astype(BF16),
            "w_out": w_out[l].astype(BF16),
            "ffn2_norm": row(ffn2_norm[l]), "ffn2_w_in": ffn2_w_in[l].astype(BF16),
            "ffn2_w_out": ffn2_w_out[l].astype(BF16), "final_norm": row(final_norm[l]),
        }
        shift, single_pass = _logit_shift(lp["qg"], lp["kg"], rel_bias)
        tiles, tiles_t, lam = _prep(rel_bias.astype(F32), shift, row(lambda_q1[l]), row(lambda_k1[l]),
                                    row(lambda_q2[l]), row(lambda_k2[l]), lambda_init)
        out_scale = 1.0 - lambda_init
        y_prompt = _layer(y_prompt, lp, tiles, tiles_t, lam, out_scale, single_pass)
        y_sample = _layer(y_sample, lp, tiles, tiles_t, lam, out_scale, single_pass)
    return (y_prompt, y_sample)
```

```python
import functools
import math

import jax
import jax.numpy as jnp
from jax import lax
from jax.experimental import pallas as pl
from jax.experimental.pallas import tpu as pltpu

D_MODEL = 1024
D_FF = 2816
CHUNK = 128
A_WIDTH = 512
A_GROUPS = 8
A_GROUP_DIM = A_WIDTH // A_GROUPS
B_HEADS = 4
HEAD_DIM = 64
HEAD_WIDTH = 2 * HEAD_DIM
B_QK = B_HEADS * HEAD_WIDTH
B_V = B_HEADS * HEAD_WIDTH
N_BUCKETS = 32
MAX_DISTANCE = 128
IN_COLS = 2 * A_WIDTH + 2 * B_QK + B_V + 2 * D_MODEL
EPS = 1e-6

V7X_LANES = 128
V7X_MXU_DIM = 256
V7X_VMEM_LIMIT_BYTES = 56 * 1024 * 1024

ROW_TILE = 512
WIDE_ROW_TILE = 1024
SUB_TILE_ROWS = 256
ATTN_TILE = 256
ATTN_KEY_STEP = 512
LOG2E = math.log2(math.e)
LOGIT_SPAN = 100.0
N_BIAS_TILES = 5

F32 = jnp.float32
BF16 = jnp.bfloat16


def _rms(x, g):
    return x * lax.rsqrt(jnp.mean(x * x, axis=-1, keepdims=True) + EPS) * g


def _const_spec(shape):
    return pl.BlockSpec(shape, lambda *_: (0,) * len(shape), pipeline_mode=pl.Buffered(1))


def _params(n_axes):
    return pltpu.CompilerParams(dimension_semantics=("parallel",) * n_axes,
                                vmem_limit_bytes=V7X_VMEM_LIMIT_BYTES)


def _macaron_half_step(x, g, win_ref, wout_ref):
    h = _rms(x, g).astype(BF16)
    gu = jnp.dot(h, win_ref[...], preferred_element_type=F32)
    gate = gu[:, :D_FF]
    up = gu[:, D_FF:]
    act = (gate * jax.nn.sigmoid(gate) * up).astype(BF16)
    return x + 0.5 * jnp.dot(act, wout_ref[...], preferred_element_type=F32)


def _sub_tiles(n_rows):
    return [slice(r, r + SUB_TILE_ROWS) for r in range(0, n_rows, SUB_TILE_ROWS)]


def _ffn_body(x_ref, g_ref, win_ref, wout_ref, o_ref):
    for rows in _sub_tiles(x_ref.shape[0]):
        o_ref[rows, :] = _macaron_half_step(x_ref[rows, :], g_ref[...], win_ref, wout_ref)


def _ffn(x, g, w_in, w_out):
    n = x.shape[0]
    row = pl.BlockSpec((WIDE_ROW_TILE, D_MODEL), lambda i: (i, 0))
    return pl.pallas_call(
        _ffn_body,
        grid=(n // WIDE_ROW_TILE,),
        in_specs=[row, _const_spec((1, D_MODEL)), _const_spec((D_MODEL, 2 * D_FF)),
                  _const_spec((D_FF, D_MODEL))],
        out_specs=row,
        out_shape=jax.ShapeDtypeStruct((n, D_MODEL), F32),
        compiler_params=_params(1),
        name="ffn",
    )(x, g, w_in, w_out)


def _proj_body(x_ref, *refs):
    params, outs = refs[:9], refs[9:]
    for rows in _sub_tiles(x_ref.shape[0]):
        _proj_rows(x_ref.at[rows], *params, *(o.at[rows] for o in outs))


def _proj_rows(x_ref, g_ref, win_ref, gbias_ref, sgug_ref, wcat_ref, sbias_ref, seg_ref,
               qg_ref, kg_ref, a_ref, q_ref, k_ref, v_ref, gate_ref):
    tm = x_ref.shape[0]
    h = _rms(x_ref[...], g_ref[...]).astype(BF16)
    proj = jnp.dot(h, win_ref[...], preferred_element_type=F32)
    c0 = 2 * A_WIDTH
    c1 = c0 + B_QK
    c2 = c1 + B_QK
    c3 = c2 + B_V

    uv = jax.nn.gelu(proj[:, :c0])
    u = uv[:, :A_WIDTH]
    vn = _rms(uv[:, A_WIDTH:], sgug_ref[...])
    lane = lax.broadcasted_iota(jnp.int32, (tm, A_WIDTH), 1)
    low = (lane & (V7X_LANES - 1)) < A_GROUP_DIM
    v_low = jnp.where(low, vn, 0.0).astype(BF16)
    v_high = jnp.where(low, 0.0, vn).astype(BF16)
    for c in range(0, tm // CHUNK, 2):
        pair = [slice((c + i) * CHUNK, (c + i + 1) * CHUNK) for i in range(2)]
        slabs = []
        for p in range(A_WIDTH // V7X_LANES):
            cols = slice(p * V7X_LANES, (p + 1) * V7X_LANES)
            rhs = jnp.concatenate(
                [jnp.concatenate([v_low[rows, cols], v_high[rows, cols]], axis=0)
                 for rows in pair], axis=1)
            slabs.append(jnp.dot(wcat_ref[p], rhs, preferred_element_type=F32))
        for i, rows in enumerate(pair):
            s = jnp.concatenate([slab[:, i * V7X_LANES:(i + 1) * V7X_LANES] for slab in slabs],
                                axis=1) + sbias_ref[...]
            a_ref[rows, :] = (u[rows, :] * s).astype(a_ref.dtype)

    def seg_norm(t, gain):
        sq = (t * t).astype(BF16)
        half = seg_ref.shape[0]
        ms = jnp.concatenate(
            [jnp.dot(sq[:, i * half:(i + 1) * half], seg_ref[...], preferred_element_type=F32)
             for i in range(B_QK // half)], axis=1)
        return (t * lax.rsqrt(ms + EPS) * gain).astype(BF16)

    q_ref[...] = seg_norm(proj[:, c0:c1], qg_ref[...])
    k_ref[...] = seg_norm(proj[:, c1:c2], kg_ref[...])
    v_ref[...] = proj[:, c2:c3].astype(BF16)
    gate_ref[...] = jax.nn.sigmoid(proj[:, c3:] + gbias_ref[...]).astype(gate_ref.dtype)


def _proj(x, g, w_in, gate_bias, sgu_g, wcat, sbias, seg, qg, kg):
    n = x.shape[0]
    row = lambda w: pl.BlockSpec((WIDE_ROW_TILE, w), lambda i: (i, 0))
    out_w = (A_WIDTH, B_QK, B_QK, B_V, 2 * D_MODEL)
    return pl.pallas_call(
        _proj_body,
        grid=(n // WIDE_ROW_TILE,),
        in_specs=[row(D_MODEL), _const_spec((1, D_MODEL)), _const_spec((D_MODEL, IN_COLS)),
                  _const_spec((1, 2 * D_MODEL)), _const_spec((1, A_WIDTH)),
                  _const_spec(wcat.shape), _const_spec(sbias.shape), _const_spec(seg.shape),
                  _const_spec((1, B_QK)), _const_spec((1, B_QK))],
        out_specs=[row(w) for w in out_w],
        out_shape=[jax.ShapeDtypeStruct((n, w), BF16) for w in out_w],
        compiler_params=_params(1),
        name="proj",
    )(x, g, w_in, gate_bias, sgu_g, wcat, sbias, seg, qg, kg)


def _bucket_thresholds():
    half = N_BUCKETS // 2
    max_exact = half // 2
    n_log = half - max_exact
    ratio = MAX_DISTANCE // max_exact
    out = []
    for j in range(1, n_log):
        target = max_exact ** n_log * ratio ** j
        n = max_exact
        while n ** n_log < target:
            n += 1
        out.append(n)
    return max_exact, out


def _prep_body(rb_ref, shift_ref, lq1_ref, lk1_ref, lq2_ref, lk2_ref, tiles_ref, tiles_t_ref,
               lam_ref, *, lambda_init):
    t = ATTN_TILE
    half = N_BUCKETS // 2
    max_exact, thresholds = _bucket_thresholds()
    row = lax.broadcasted_iota(jnp.int32, (t, t), 0)
    col = lax.broadcasted_iota(jnp.int32, (t, t), 1)
    shift = shift_ref[0, 0]
    for hd in range(B_HEADS):
        for idx, b in ((0, half - 1), (N_BIAS_TILES - 1, N_BUCKETS - 1)):
            far = jnp.full((t, t), rb_ref[b, hd] * LOG2E - shift, F32)
            tiles_ref[hd, idx] = far
            tiles_t_ref[hd, idx] = far
    for d in (-1, 0, 1):
        rel = col - row + d * t
        n = jnp.abs(rel)
        large = jnp.full((t, t), max_exact, jnp.int32)
        for th in thresholds:
            large = large + jnp.where(n >= th, 1, 0)
        bucket = jnp.where(rel > 0, half, 0) + jnp.where(n < max_exact, n, large)
        for hd in range(B_HEADS):
            tile = jnp.zeros((t, t), F32)
            for b in range(N_BUCKETS):
                tile = jnp.where(bucket == b, rb_ref[b, hd] * LOG2E - shift, tile)
            tiles_ref[hd, d + N_BIAS_TILES // 2] = tile
            tiles_t_ref[hd, d + N_BIAS_TILES // 2] = tile.T
    s1 = jnp.sum(lq1_ref[...] * lk1_ref[...], axis=-1, keepdims=True)
    s2 = jnp.sum(lq2_ref[...] * lk2_ref[...], axis=-1, keepdims=True)
    lam = jnp.exp(s1) - jnp.exp(s2) + lambda_init
    lam_ref[...] = jnp.broadcast_to(lam, lam_ref.shape)


def _prep(rel_bias, shift, lq1, lk1, lq2, lk2, lambda_init):
    t = ATTN_TILE
    vec = pl.BlockSpec((1, HEAD_DIM), lambda: (0, 0))
    return pl.pallas_call(
        functools.partial(_prep_body, lambda_init=lambda_init),
        in_specs=[pl.BlockSpec(memory_space=pltpu.SMEM), pl.BlockSpec(memory_space=pltpu.SMEM),
                  vec, vec, vec, vec],
        out_specs=[pl.BlockSpec((B_HEADS, N_BIAS_TILES, t, t), lambda: (0, 0, 0, 0)),
                   pl.BlockSpec((B_HEADS, N_BIAS_TILES, t, t), lambda: (0, 0, 0, 0)),
                   pl.BlockSpec((8, V7X_LANES), lambda: (0, 0))],
        out_shape=[jax.ShapeDtypeStruct((B_HEADS, N_BIAS_TILES, t, t), F32),
                   jax.ShapeDtypeStruct((B_HEADS, N_BIAS_TILES, t, t), F32),
                   jax.ShapeDtypeStruct((8, V7X_LANES), F32)],
        name="prep",
    )(rel_bias, shift, lq1, lk1, lq2, lk2)


def _q_stack(q_ref, c):
    t = ATTN_TILE
    lane = lax.broadcasted_iota(jnp.int32, (t, HEAD_WIDTH), 1)
    first = jnp.where(lane < HEAD_DIM, 1.0, 0.0).astype(BF16)
    q = q_ref[pl.ds(pl.multiple_of(c * t, t), t), :]
    return jnp.concatenate([q * first, q * (1.0 - first)], axis=0)


def _biased_logits(q2, c, j, k_ref, tiles_ref):
    t = ATTN_TILE
    ks = ATTN_KEY_STEP
    bias = jnp.concatenate(
        [tiles_ref[jnp.clip(jt - c, -2, 2) + N_BIAS_TILES // 2]
         for jt in range(j * ks // t, (j + 1) * ks // t)], axis=1)
    s = lax.dot_general(q2, k_ref[j * ks:(j + 1) * ks, :], (((1,), (1,)), ((), ())),
                        preferred_element_type=F32)
    return s + jnp.concatenate([bias, bias], axis=0)


def _store_output(res, c, lam_ref, subg_ref, o_ref, out_scale):
    t = ATTN_TILE
    out = res[:t] - lam_ref[0:1, 0:1] * res[t:]
    o_ref[pl.ds(pl.multiple_of(c * t, t), t), :] = (
        _rms(out, subg_ref[...]) * out_scale).astype(o_ref.dtype)


def _attn_exact_body(lam_ref, q_ref, k_ref, v_ref, tiles_ref, subg_ref, o_ref,
                     s_ref, m_ref, mnext_ref, l_ref, acc_ref, *, out_scale):
    t = ATTN_TILE
    ks = ATTN_KEY_STEP
    nk = k_ref.shape[0] // ks
    nq = q_ref.shape[0] // t
    reps = ks // V7X_LANES

    def logits_step(c, q2, j):
        s = _biased_logits(q2, c, j, k_ref, tiles_ref)
        s_ref[:, j * ks:(j + 1) * ks] = s
        blockmax = s[:, :V7X_LANES]
        for r in range(1, reps):
            blockmax = jnp.maximum(blockmax, s[:, r * V7X_LANES:(r + 1) * V7X_LANES])
        if j == 0:
            mnext_ref[...] = blockmax
        else:
            mnext_ref[...] = jnp.maximum(mnext_ref[...], blockmax)

    def finish_logits():
        m = jnp.max(mnext_ref[...], axis=1, keepdims=True)
        m_ref[...] = jnp.broadcast_to(m, m_ref.shape)

    def prob_step(j, s):
        p = jnp.exp2(s - jnp.concatenate([m_ref[...]] * reps, axis=1))
        psum = p[:, :V7X_LANES]
        for r in range(1, reps):
            psum = psum + p[:, r * V7X_LANES:(r + 1) * V7X_LANES]
        pv = jnp.dot(p.astype(BF16), v_ref[j * ks:(j + 1) * ks, :], preferred_element_type=F32)
        if j == 0:
            l_ref[...] = psum
            acc_ref[...] = pv
        else:
            l_ref[...] += psum
            acc_ref[...] += pv

    def finish_block(c):
        l = jnp.sum(l_ref[...], axis=1, keepdims=True)
        _store_output(acc_ref[...] / l, c, lam_ref, subg_ref, o_ref, out_scale)

    q2 = _q_stack(q_ref, 0)
    for j in range(nk):
        logits_step(0, q2, j)
    finish_logits()

    def body(c, carry):
        q2n = _q_stack(q_ref, c + 1)
        for j in range(nk):
            s = s_ref[:, j * ks:(j + 1) * ks]
            logits_step(c + 1, q2n, j)
            prob_step(j, s)
        finish_block(c)
        finish_logits()
        return carry

    lax.fori_loop(0, nq - 1, body, 0)

    for j in range(nk):
        prob_step(j, s_ref[:, j * ks:(j + 1) * ks])
    finish_block(nq - 1)


def _attn_fast_body(lam_ref, q_ref, k_ref, v_ref, tiles_t_ref, subg_ref, o_ref, pt_ref, vt_ref,
                    l_ref, *, out_scale):
    t = ATTN_TILE
    ks = ATTN_KEY_STEP
    seq = k_ref.shape[0]
    nk = seq // ks
    nq = q_ref.shape[0] // t

    vt_ref[...] = v_ref[...].astype(F32).T.astype(BF16)

    def prob_block(c):
        q2 = _q_stack(q_ref, c)
        total = None
        for j in range(nk):
            keys = slice(j * ks, (j + 1) * ks)
            bias = jnp.concatenate(
                [tiles_t_ref[jnp.clip(jt - c, -2, 2) + N_BIAS_TILES // 2]
                 for jt in range(j * ks // t, (j + 1) * ks // t)], axis=0)
            s = lax.dot_general(k_ref[keys, :], q2, (((1,), (1,)), ((), ())),
                                preferred_element_type=F32)
            p = jnp.exp2(s + jnp.concatenate([bias, bias], axis=1))
            part = jnp.sum(p, axis=0, keepdims=True)
            total = part if total is None else total + part
            pt_ref[keys, :] = p.astype(BF16)
        l_ref[...] = jnp.broadcast_to(total, l_ref.shape)

    def output_block(c):
        res = jnp.dot(vt_ref[...], pt_ref[...], preferred_element_type=F32)
        res = res / l_ref[0:1, :]
        out = res[:, :t] - lam_ref[0:1, 0:1] * res[:, t:]
        out = out * lax.rsqrt(jnp.mean(out * out, axis=0, keepdims=True) + EPS)
        o_ref[pl.ds(pl.multiple_of(c * t, t), t), :] = (
            out.T * subg_ref[...] * out_scale).astype(o_ref.dtype)

    prob_block(0)

    def body(c, carry):
        output_block(c - 1)
        prob_block(c)
        return carry

    lax.fori_loop(1, nq, body, 0, unroll=4)
    output_block(nq - 1)


def _attn(lam, q, k, v, tiles, tiles_t, subg, *, out_scale, single_pass):
    b, s, _ = q.shape
    t = ATTN_TILE
    seq = pl.BlockSpec((None, s, HEAD_WIDTH), lambda bi, hi: (bi, 0, hi))
    stack = pltpu.VMEM((2 * t, V7X_LANES), F32)
    if single_pass:
        body, name, bias_tiles = _attn_fast_body, "attn_fast", tiles_t
        scratch = [pltpu.VMEM((s, 2 * t), BF16), pltpu.VMEM((HEAD_WIDTH, s), BF16),
                   pltpu.VMEM((8, 2 * t), F32)]
    else:
        body, name, bias_tiles = _attn_exact_body, "attn_exact", tiles
        scratch = [pltpu.VMEM((2 * t, s), F32), stack, stack, stack, stack]
    return pl.pallas_call(
        functools.partial(body, out_scale=out_scale),
        grid=(b, B_HEADS),
        in_specs=[pl.BlockSpec((8, V7X_LANES), lambda bi, hi: (0, 0)),
                  seq, seq, seq,
                  pl.BlockSpec((None, N_BIAS_TILES, t, t), lambda bi, hi: (hi, 0, 0, 0)),
                  pl.BlockSpec((1, HEAD_WIDTH), lambda bi, hi: (0, 0))],
        out_specs=seq,
        out_shape=jax.ShapeDtypeStruct((b, s, B_V), BF16),
        scratch_shapes=scratch,
        compiler_params=_params(2),
        name=name,
    )(lam, q, k, v, bias_tiles, subg)


def _logit_shift(q_gain, k_gain, rel_bias):
    qk_bound = 1.02 * HEAD_DIM * jnp.max(jnp.abs(q_gain)) * jnp.max(jnp.abs(k_gain))
    bias = rel_bias.astype(F32) * LOG2E
    shift = qk_bound + jnp.max(bias)
    span = 2.0 * qk_bound + jnp.max(bias) - jnp.min(bias)
    return shift.reshape(1, 1), span <= LOGIT_SPAN


def _merge_ffn_body(x_ref, a_ref, b_ref, gate_ref, wpa_ref, wpb_ref, wo_ref, g_ref, win_ref,
                    wout_ref, fg_ref, o_ref):
    ga = gate_ref[:, :D_MODEL].astype(F32)
    gb = gate_ref[:, D_MODEL:].astype(F32)
    merged = (ga * jnp.dot(a_ref[...], wpa_ref[...], preferred_element_type=F32)
              + gb * jnp.dot(b_ref[...], wpb_ref[...], preferred_element_type=F32))
    x = x_ref[...] + jnp.dot(merged.astype(BF16), wo_ref[...], preferred_element_type=F32)
    for rows in _sub_tiles(x.shape[0]):
        o_ref[rows, :] = _rms(_macaron_half_step(x[rows, :], g_ref[...], win_ref, wout_ref),
                              fg_ref[...])


def _merge_ffn(x, a, b, gates, wpa, wpb, wo, g, w_in, w_out, final_g):
    n = x.shape[0]
    row = lambda w: pl.BlockSpec((ROW_TILE, w), lambda i: (i, 0))
    return pl.pallas_call(
        _merge_ffn_body,
        grid=(n // ROW_TILE,),
        in_specs=[row(D_MODEL), row(A_WIDTH), row(B_V), row(2 * D_MODEL),
                  _const_spec((A_WIDTH, D_MODEL)), _const_spec((B_V, D_MODEL)),
                  _const_spec((D_MODEL, D_MODEL)), _const_spec((1, D_MODEL)),
                  _const_spec((D_MODEL, 2 * D_FF)), _const_spec((D_FF, D_MODEL)),
                  _const_spec((1, D_MODEL))],
        out_specs=row(D_MODEL),
        out_shape=jax.ShapeDtypeStruct((n, D_MODEL), F32),
        compiler_params=_params(1),
        name="merge_ffn",
    )(x, a, b, gates, wpa, wpb, wo, g, w_in, w_out, final_g)


def _layer(x, lp, tiles, tiles_t, lam, out_scale, single_pass):
    b, s, _ = x.shape
    x0 = x.reshape(b * s, D_MODEL)
    x1 = _ffn(x0, lp["ffn1_norm"], lp["ffn1_w_in"], lp["ffn1_w_out"])
    a, q, k, v, gates = _proj(x1, lp["mix_norm"], lp["w_in"], lp["gate_bias"], lp["sgu_norm"],
                              lp["wcat"], lp["sbias"], lp["seg"], lp["qg"], lp["kg"])
    bo = lax.cond(single_pass,
                  functools.partial(_attn, out_scale=out_scale, single_pass=True),
                  functools.partial(_attn, out_scale=out_scale, single_pass=False),
                  lam, q.reshape(b, s, B_QK), k.reshape(b, s, B_QK), v.reshape(b, s, B_V),
                  tiles, tiles_t, lp["diff_subln"])
    y = _merge_ffn(x1, a, bo.reshape(b * s, B_V), gates, lp["w_proj_a"], lp["w_proj_b"], lp["w_out"],
                   lp["ffn2_norm"], lp["ffn2_w_in"], lp["ffn2_w_out"], lp["final_norm"])
    return y.reshape(b, s, D_MODEL)


def kernel(x_prompt, x_sample, rel_bias, ffn1_norm, ffn1_w_in, ffn1_w_out, mix_norm, w_in, gate_bias, sgu_norm, sgu_w, sgu_b, q_norm, k_norm, lambda_q1, lambda_k1, lambda_q2, lambda_k2, diff_subln, w_proj_a, w_proj_b, w_out, ffn2_norm, ffn2_w_in, ffn2_w_out, final_norm):
    depth = ffn1_norm.shape[0]
    y_prompt, y_sample = x_prompt, x_sample
    row = lambda p: p.reshape(1, -1).astype(F32)
    for l in range(depth):
        lambda_init = 0.8 - 0.6 * math.exp(-0.3 * l)
        sw = sgu_w[l].astype(BF16)
        lp = {
            "ffn1_norm": row(ffn1_norm[l]), "ffn1_w_in": ffn1_w_in[l].astype(BF16),
            "ffn1_w_out": ffn1_w_out[l].astype(BF16),
            "mix_norm": row(mix_norm[l]), "w_in": w_in[l].astype(BF16),
            "gate_bias": row(gate_bias[l]), "sgu_norm": row(sgu_norm[l]),
            "wcat": jnp.concatenate([sw[0::2], sw[1::2]], axis=-1),
            "sbias": jnp.repeat(sgu_b[l].T.astype(F32), A_GROUP_DIM, axis=1),
            "seg": jnp.kron(jnp.eye(V7X_MXU_DIM // HEAD_DIM, dtype=F32),
                            jnp.full((HEAD_DIM, HEAD_DIM), 1.0 / HEAD_DIM, F32)).astype(BF16),
            "qg": jnp.tile(row(q_norm[l]) * (HEAD_DIM ** -0.5 * LOG2E), (1, B_QK // HEAD_DIM)),
            "kg": jnp.tile(row(k_norm[l]), (1, B_QK // HEAD_DIM)),
            "diff_subln": row(diff_subln[l]),
            "w_proj_a": w_proj_a[l].astype(BF16), "w_proj_b": w_proj_b[l].astype(BF16),
            "w_out": w_out[l].astype(BF16),
            "ffn2_norm": row(ffn2_norm[l]), "ffn2_w_in": ffn2_w_in[l].astype(BF16),
            "ffn2_w_out": ffn2_w_out[l].astype(BF16), "final_norm": row(final_norm[l]),
        }
        shift, single_pass = _logit_shift(lp["qg"], lp["kg"], rel_bias)
        tiles, tiles_t, lam = _prep(rel_bias.astype(F32), shift, row(lambda_q1[l]), row(lambda_k1[l]),
                                    row(lambda_q2[l]), row(lambda_k2[l]), lambda_init)
        out_scale = 1.0 - lambda_init
        y_prompt = _layer(y_prompt, lp, tiles, tiles_t, lam, out_scale, single_pass)
        y_sample = _layer(y_sample, lp, tiles, tiles_t, lam, out_scale, single_pass)
    return (y_prompt, y_sample)
```

```python
import functools
import math

import jax
import jax.numpy as jnp
from jax import lax
from jax.experimental import pallas as pl
from jax.experimental.pallas import tpu as pltpu

D_MODEL = 1024
D_FF = 2816
CHUNK = 128
A_WIDTH = 512
A_GROUPS = 8
A_GROUP_DIM = A_WIDTH // A_GROUPS
B_HEADS = 4
HEAD_DIM = 64
HEAD_WIDTH = 2 * HEAD_DIM
B_QK = B_HEADS * HEAD_WIDTH
B_V = B_HEADS * HEAD_WIDTH
N_BUCKETS = 32
MAX_DISTANCE = 128
IN_COLS = 2 * A_WIDTH + 2 * B_QK + B_V + 2 * D_MODEL
EPS = 1e-6

V7X_LANES = 128
V7X_MXU_DIM = 256
V7X_VMEM_LIMIT_BYTES = 56 * 1024 * 1024

ROW_TILE = 512
WIDE_ROW_TILE = 1024
SUB_TILE_ROWS = 256
ATTN_TILE = 256
ATTN_KEY_STEP = 512
LOG2E = math.log2(math.e)
LOGIT_SPAN = 100.0
N_BIAS_TILES = 5

F32 = jnp.float32
BF16 = jnp.bfloat16


def _rms(x, g):
    return x * lax.rsqrt(jnp.mean(x * x, axis=-1, keepdims=True) + EPS) * g


def _const_spec(shape):
    return pl.BlockSpec(shape, lambda *_: (0,) * len(shape), pipeline_mode=pl.Buffered(1))


def _params(n_axes):
    return pltpu.CompilerParams(dimension_semantics=("parallel",) * n_axes,
                                vmem_limit_bytes=V7X_VMEM_LIMIT_BYTES)


def _macaron_half_step(x, g, win_ref, wout_ref):
    h = _rms(x, g).astype(BF16)
    gu = jnp.dot(h, win_ref[...], preferred_element_type=F32)
    gate = gu[:, :D_FF]
    up = gu[:, D_FF:]
    act = (gate * jax.nn.sigmoid(gate) * up).astype(BF16)
    return x + 0.5 * jnp.dot(act, wout_ref[...], preferred_element_type=F32)


def _sub_tiles(n_rows):
    return [slice(r, r + SUB_TILE_ROWS) for r in range(0, n_rows, SUB_TILE_ROWS)]


def _ffn_body(x_ref, g_ref, win_ref, wout_ref, o_ref):
    for rows in _sub_tiles(x_ref.shape[0]):
        o_ref[rows, :] = _macaron_half_step(x_ref[rows, :], g_ref[...], win_ref, wout_ref)


def _ffn(x, g, w_in, w_out):
    n = x.shape[0]
    row = pl.BlockSpec((WIDE_ROW_TILE, D_MODEL), lambda i: (i, 0))
    return pl.pallas_call(
        _ffn_body,
        grid=(n // WIDE_ROW_TILE,),
        in_specs=[row, _const_spec((1, D_MODEL)), _const_spec((D_MODEL, 2 * D_FF)),
                  _const_spec((D_FF, D_MODEL))],
        out_specs=row,
        out_shape=jax.ShapeDtypeStruct((n, D_MODEL), F32),
        compiler_params=_params(1),
        name="ffn",
    )(x, g, w_in, w_out)


def _proj_body(x_ref, *refs):
    n_outs = 5
    params, outs = refs[:-n_outs], refs[-n_outs:]
    for rows in _sub_tiles(x_ref.shape[0]):
        _proj_rows(x_ref.at[rows], *params, *(o.at[rows] for o in outs))


def _proj_rows(x_ref, g_ref, win_ref, gbias_ref, sgug_ref, wcat_ref, sbias_ref, seg_ref,
               qg_ref, kg_ref, a_ref, q_ref, k_ref, v_ref, gate_ref):
    tm = x_ref.shape[0]
    h = _rms(x_ref[...], g_ref[...]).astype(BF16)
    proj = jnp.dot(h, win_ref[...], preferred_element_type=F32)
    c0 = 2 * A_WIDTH
    c1 = c0 + B_QK
    c2 = c1 + B_QK
    c3 = c2 + B_V

    uv = jax.nn.gelu(proj[:, :c0])
    u = uv[:, :A_WIDTH]
    vn = _rms(uv[:, A_WIDTH:], sgug_ref[...])
    lane = lax.broadcasted_iota(jnp.int32, (tm, A_WIDTH), 1)
    low = (lane & (V7X_LANES - 1)) < A_GROUP_DIM
    v_low = jnp.where(low, vn, 0.0).astype(BF16)
    v_high = jnp.where(low, 0.0, vn).astype(BF16)
    for c in range(0, tm // CHUNK, 2):
        pair = [slice((c + i) * CHUNK, (c + i + 1) * CHUNK) for i in range(2)]
        slabs = []
        for p in range(A_WIDTH // V7X_LANES):
            cols = slice(p * V7X_LANES, (p + 1) * V7X_LANES)
            rhs = jnp.concatenate(
                [jnp.concatenate([v_low[rows, cols], v_high[rows, cols]], axis=0)
                 for rows in pair], axis=1)
            slabs.append(jnp.dot(wcat_ref[p], rhs, preferred_element_type=F32))
        for i, rows in enumerate(pair):
            s = jnp.concatenate([slab[:, i * V7X_LANES:(i + 1) * V7X_LANES] for slab in slabs],
                                axis=1) + sbias_ref[...]
            a_ref[rows, :] = (u[rows, :] * s).astype(a_ref.dtype)

    def seg_norm(t, gain):
        sq = (t * t).astype(BF16)
        half = seg_ref.shape[0]
        ms = jnp.concatenate(
            [jnp.dot(sq[:, i * half:(i + 1) * half], seg_ref[...], preferred_element_type=F32)
             for i in range(B_QK // half)], axis=1)
        return (t * lax.rsqrt(ms + EPS) * gain).astype(BF16)

    q_ref[...] = seg_norm(proj[:, c0:c1], qg_ref[...])
    k_ref[...] = seg_norm(proj[:, c1:c2], kg_ref[...])
    v_ref[...] = proj[:, c2:c3].astype(BF16)
    gate_ref[...] = jax.nn.sigmoid(proj[:, c3:] + gbias_ref[...]).astype(gate_ref.dtype)


def _proj(x, g, w_in, gate_bias, sgu_g, wcat, sbias, seg, qg, kg):
    n = x.shape[0]
    row = lambda w: pl.BlockSpec((WIDE_ROW_TILE, w), lambda i: (i, 0))
    out_w = (A_WIDTH, B_QK, B_QK, B_V, 2 * D_MODEL)
    return pl.pallas_call(
        _proj_body,
        grid=(n // WIDE_ROW_TILE,),
        in_specs=[row(D_MODEL), _const_spec((1, D_MODEL)), _const_spec((D_MODEL, IN_COLS)),
                  _const_spec((1, 2 * D_MODEL)), _const_spec((1, A_WIDTH)),
                  _const_spec(wcat.shape), _const_spec(sbias.shape), _const_spec(seg.shape),
                  _const_spec((1, B_QK)), _const_spec((1, B_QK))],
        out_specs=[row(w) for w in out_w],
        out_shape=[jax.ShapeDtypeStruct((n, w), BF16) for w in out_w],
        compiler_params=_params(1),
        name="proj",
    )(x, g, w_in, gate_bias, sgu_g, wcat, sbias, seg, qg, kg)


def _bucket_thresholds():
    half = N_BUCKETS // 2
    max_exact = half // 2
    n_log = half - max_exact
    ratio = MAX_DISTANCE // max_exact
    out = []
    for j in range(1, n_log):
        target = max_exact ** n_log * ratio ** j
        n = max_exact
        while n ** n_log < target:
            n += 1
        out.append(n)
    return max_exact, out


def _prep_body(rb_ref, shift_ref, lq1_ref, lk1_ref, lq2_ref, lk2_ref, tiles_ref, tiles_t_ref,
               lam_ref, *, lambda_init):
    t = ATTN_TILE
    half = N_BUCKETS // 2
    max_exact, thresholds = _bucket_thresholds()
    row = lax.broadcasted_iota(jnp.int32, (t, t), 0)
    col = lax.broadcasted_iota(jnp.int32, (t, t), 1)
    shift = shift_ref[0, 0]
    for hd in range(B_HEADS):
        for idx, b in ((0, half - 1), (N_BIAS_TILES - 1, N_BUCKETS - 1)):
            far = jnp.full((t, t), rb_ref[b, hd] * LOG2E - shift, F32)
            tiles_ref[hd, idx] = far
            tiles_t_ref[hd, idx] = far
    for d in (-1, 0, 1):
        rel = col - row + d * t
        n = jnp.abs(rel)
        large = jnp.full((t, t), max_exact, jnp.int32)
        for th in thresholds:
            large = large + jnp.where(n >= th, 1, 0)
        bucket = jnp.where(rel > 0, half, 0) + jnp.where(n < max_exact, n, large)
        for hd in range(B_HEADS):
            tile = jnp.zeros((t, t), F32)
            for b in range(N_BUCKETS):
                tile = jnp.where(bucket == b, rb_ref[b, hd] * LOG2E - shift, tile)
            tiles_ref[hd, d + N_BIAS_TILES // 2] = tile
            tiles_t_ref[hd, d + N_BIAS_TILES // 2] = tile.T
    s1 = jnp.sum(lq1_ref[...] * lk1_ref[...], axis=-1, keepdims=True)
    s2 = jnp.sum(lq2_ref[...] * lk2_ref[...], axis=-1, keepdims=True)
    lam = jnp.exp(s1) - jnp.exp(s2) + lambda_init
    lam_ref[...] = jnp.broadcast_to(lam, lam_ref.shape)


def _prep(rel_bias, shift, lq1, lk1, lq2, lk2, lambda_init):
    t = ATTN_TILE
    vec = pl.BlockSpec((1, HEAD_DIM), lambda: (0, 0))
    return pl.pallas_call(
        functools.partial(_prep_body, lambda_init=lambda_init),
        in_specs=[pl.BlockSpec(memory_space=pltpu.SMEM), pl.BlockSpec(memory_space=pltpu.SMEM),
                  vec, vec, vec, vec],
        out_specs=[pl.BlockSpec((B_HEADS, N_BIAS_TILES, t, t), lambda: (0, 0, 0, 0)),
                   pl.BlockSpec((B_HEADS, N_BIAS_TILES, t, t), lambda: (0, 0, 0, 0)),
                   pl.BlockSpec((8, V7X_LANES), lambda: (0, 0))],
        out_shape=[jax.ShapeDtypeStruct((B_HEADS, N_BIAS_TILES, t, t), F32),
                   jax.ShapeDtypeStruct((B_HEADS, N_BIAS_TILES, t, t), F32),
                   jax.ShapeDtypeStruct((8, V7X_LANES), F32)],
        name="prep",
    )(rel_bias, shift, lq1, lk1, lq2, lk2)


def _q_stack(q_ref, c):
    t = ATTN_TILE
    lane = lax.broadcasted_iota(jnp.int32, (t, HEAD_WIDTH), 1)
    first = jnp.where(lane < HEAD_DIM, 1.0, 0.0).astype(BF16)
    q = q_ref[pl.ds(pl.multiple_of(c * t, t), t), :]
    return jnp.concatenate([q * first, q * (1.0 - first)], axis=0)


def _biased_logits(q2, c, j, k_ref, tiles_ref):
    t = ATTN_TILE
    ks = ATTN_KEY_STEP
    bias = jnp.concatenate(
        [tiles_ref[jnp.clip(jt - c, -2, 2) + N_BIAS_TILES // 2]
         for jt in range(j * ks // t, (j + 1) * ks // t)], axis=1)
    s = lax.dot_general(q2, k_ref[j * ks:(j + 1) * ks, :], (((1,), (1,)), ((), ())),
                        preferred_element_type=F32)
    return s + jnp.concatenate([bias, bias], axis=0)


def _store_output(res, c, lam_ref, subg_ref, o_ref, out_scale):
    t = ATTN_TILE
    out = res[:t] - lam_ref[0:1, 0:1] * res[t:]
    o_ref[pl.ds(pl.multiple_of(c * t, t), t), :] = (
        _rms(out, subg_ref[...]) * out_scale).astype(o_ref.dtype)


def _attn_exact_body(lam_ref, q_ref, k_ref, v_ref, tiles_ref, subg_ref, o_ref,
                     s_ref, m_ref, mnext_ref, l_ref, acc_ref, *, out_scale):
    t = ATTN_TILE
    ks = ATTN_KEY_STEP
    nk = k_ref.shape[0] // ks
    nq = q_ref.shape[0] // t
    reps = ks // V7X_LANES

    def logits_step(c, q2, j):
        s = _biased_logits(q2, c, j, k_ref, tiles_ref)
        s_ref[:, j * ks:(j + 1) * ks] = s
        blockmax = s[:, :V7X_LANES]
        for r in range(1, reps):
            blockmax = jnp.maximum(blockmax, s[:, r * V7X_LANES:(r + 1) * V7X_LANES])
        if j == 0:
            mnext_ref[...] = blockmax
        else:
            mnext_ref[...] = jnp.maximum(mnext_ref[...], blockmax)

    def finish_logits():
        m = jnp.max(mnext_ref[...], axis=1, keepdims=True)
        m_ref[...] = jnp.broadcast_to(m, m_ref.shape)

    def prob_step(j, s):
        p = jnp.exp2(s - jnp.concatenate([m_ref[...]] * reps, axis=1))
        psum = p[:, :V7X_LANES]
        for r in range(1, reps):
            psum = psum + p[:, r * V7X_LANES:(r + 1) * V7X_LANES]
        pv = jnp.dot(p.astype(BF16), v_ref[j * ks:(j + 1) * ks, :], preferred_element_type=F32)
        if j == 0:
            l_ref[...] = psum
            acc_ref[...] = pv
        else:
            l_ref[...] += psum
            acc_ref[...] += pv

    def finish_block(c):
        l = jnp.sum(l_ref[...], axis=1, keepdims=True)
        _store_output(acc_ref[...] / l, c, lam_ref, subg_ref, o_ref, out_scale)

    q2 = _q_stack(q_ref, 0)
    for j in range(nk):
        logits_step(0, q2, j)
    finish_logits()

    def body(c, carry):
        q2n = _q_stack(q_ref, c + 1)
        for j in range(nk):
            s = s_ref[:, j * ks:(j + 1) * ks]
            logits_step(c + 1, q2n, j)
            prob_step(j, s)
        finish_block(c)
        finish_logits()
        return carry

    lax.fori_loop(0, nq - 1, body, 0)

    for j in range(nk):
        prob_step(j, s_ref[:, j * ks:(j + 1) * ks])
    finish_block(nq - 1)


def _attn_fast_body(lam_ref, q_ref, k_ref, v_ref, tiles_t_ref, subg_ref, o_ref, pt_ref, vt_ref,
                    l_ref, *, out_scale):
    t = ATTN_TILE
    ks = ATTN_KEY_STEP
    seq = k_ref.shape[0]
    nk = seq // ks
    nq = q_ref.shape[0] // t

    vt_ref[...] = v_ref[...].astype(F32).T.astype(BF16)

    def prob_block(c):
        q2 = _q_stack(q_ref, c)
        total = None
        for j in range(nk):
            keys = slice(j * ks, (j + 1) * ks)
            bias = jnp.concatenate(
                [tiles_t_ref[jnp.clip(jt - c, -2, 2) + N_BIAS_TILES // 2]
                 for jt in range(j * ks // t, (j + 1) * ks // t)], axis=0)
            s = lax.dot_general(k_ref[keys, :], q2, (((1,), (1,)), ((), ())),
                                preferred_element_type=F32)
            p = jnp.exp2(s + jnp.concatenate([bias, bias], axis=1))
            part = jnp.sum(p, axis=0, keepdims=True)
            total = part if total is None else total + part
            pt_ref[keys, :] = p.astype(BF16)
        l_ref[...] = jnp.broadcast_to(total, l_ref.shape)

    def output_block(c):
        res = jnp.dot(vt_ref[...], pt_ref[...], preferred_element_type=F32)
        res = res / l_ref[0:1, :]
        out = res[:, :t] - lam_ref[0:1, 0:1] * res[:, t:]
        out = out * lax.rsqrt(jnp.mean(out * out, axis=0, keepdims=True) + EPS)
        o_ref[pl.ds(pl.multiple_of(c * t, t), t), :] = (
            out.T * subg_ref[...] * out_scale).astype(o_ref.dtype)

    prob_block(0)

    def body(c, carry):
        output_block(c - 1)
        prob_block(c)
        return carry

    lax.fori_loop(1, nq, body, 0, unroll=4)
    output_block(nq - 1)


def _attn(lam, q, k, v, tiles, tiles_t, subg, *, out_scale, single_pass):
    b, s, _ = q.shape
    t = ATTN_TILE
    seq = pl.BlockSpec((None, s, HEAD_WIDTH), lambda bi, hi: (bi, 0, hi))
    stack = pltpu.VMEM((2 * t, V7X_LANES), F32)
    if single_pass:
        body, name, bias_tiles = _attn_fast_body, "attn_fast", tiles_t
        scratch = [pltpu.VMEM((s, 2 * t), BF16), pltpu.VMEM((HEAD_WIDTH, s), BF16),
                   pltpu.VMEM((8, 2 * t), F32)]
    else:
        body, name, bias_tiles = _attn_exact_body, "attn_exact", tiles
        scratch = [pltpu.VMEM((2 * t, s), F32), stack, stack, stack, stack]
    return pl.pallas_call(
        functools.partial(body, out_scale=out_scale),
        grid=(b, B_HEADS),
        in_specs=[pl.BlockSpec((8, V7X_LANES), lambda bi, hi: (0, 0)),
                  seq, seq, seq,
                  pl.BlockSpec((None, N_BIAS_TILES, t, t), lambda bi, hi: (hi, 0, 0, 0)),
                  pl.BlockSpec((1, HEAD_WIDTH), lambda bi, hi: (0, 0))],
        out_specs=seq,
        out_shape=jax.ShapeDtypeStruct((b, s, B_V), BF16),
        scratch_shapes=scratch,
        compiler_params=_params(2),
        name=name,
    )(lam, q, k, v, bias_tiles, subg)


def _logit_shift(q_gain, k_gain, rel_bias):
    qk_bound = 1.02 * HEAD_DIM * jnp.max(jnp.abs(q_gain)) * jnp.max(jnp.abs(k_gain))
    bias = rel_bias.astype(F32) * LOG2E
    shift = qk_bound + jnp.max(bias)
    span = 2.0 * qk_bound + jnp.max(bias) - jnp.min(bias)
    return shift.reshape(1, 1), span <= LOGIT_SPAN


def _merge_ffn_body(x_ref, a_ref, b_ref, gate_ref, wpa_ref, wpb_ref, wo_ref, g_ref, win_ref,
                    wout_ref, fg_ref, o_ref):
    ga = gate_ref[:, :D_MODEL].astype(F32)
    gb = gate_ref[:, D_MODEL:].astype(F32)
    merged = (ga * jnp.dot(a_ref[...], wpa_ref[...], preferred_element_type=F32)
              + gb * jnp.dot(b_ref[...], wpb_ref[...], preferred_element_type=F32))
    x = x_ref[...] + jnp.dot(merged.astype(BF16), wo_ref[...], preferred_element_type=F32)
    for rows in _sub_tiles(x.shape[0]):
        o_ref[rows, :] = _rms(_macaron_half_step(x[rows, :], g_ref[...], win_ref, wout_ref),
                              fg_ref[...])


def _merge_ffn(x, a, b, gates, wpa, wpb, wo, g, w_in, w_out, final_g):
    n = x.shape[0]
    row = lambda w: pl.BlockSpec((ROW_TILE, w), lambda i: (i, 0))
    return pl.pallas_call(
        _merge_ffn_body,
        grid=(n // ROW_TILE,),
        in_specs=[row(D_MODEL), row(A_WIDTH), row(B_V), row(2 * D_MODEL),
                  _const_spec((A_WIDTH, D_MODEL)), _const_spec((B_V, D_MODEL)),
                  _const_spec((D_MODEL, D_MODEL)), _const_spec((1, D_MODEL)),
                  _const_spec((D_MODEL, 2 * D_FF)), _const_spec((D_FF, D_MODEL)),
                  _const_spec((1, D_MODEL))],
        out_specs=row(D_MODEL),
        out_shape=jax.ShapeDtypeStruct((n, D_MODEL), F32),
        compiler_params=_params(1),
        name="merge_ffn",
    )(x, a, b, gates, wpa, wpb, wo, g, w_in, w_out, final_g)


def _layer(x, lp, tiles, tiles_t, lam, out_scale, single_pass):
    b, s, _ = x.shape
    x0 = x.reshape(b * s, D_MODEL)
    x1 = _ffn(x0, lp["ffn1_norm"], lp["ffn1_w_in"], lp["ffn1_w_out"])
    a, q, k, v, gates = _proj(x1, lp["mix_norm"], lp["w_in"], lp["gate_bias"], lp["sgu_norm"],
                              lp["wcat"], lp["sbias"], lp["seg"], lp["qg"], lp["kg"])
    bo = lax.cond(single_pass,
                  functools.partial(_attn, out_scale=out_scale, single_pass=True),
                  functools.partial(_attn, out_scale=out_scale, single_pass=False),
                  lam, q.reshape(b, s, B_QK), k.reshape(b, s, B_QK), v.reshape(b, s, B_V),
                  tiles, tiles_t, lp["diff_subln"])
    y = _merge_ffn(x1, a, bo.reshape(b * s, B_V), gates, lp["w_proj_a"], lp["w_proj_b"], lp["w_out"],
                   lp["ffn2_norm"], lp["ffn2_w_in"], lp["ffn2_w_out"], lp["final_norm"])
    return y.reshape(b, s, D_MODEL)


def kernel(x_prompt, x_sample, rel_bias, ffn1_norm, ffn1_w_in, ffn1_w_out, mix_norm, w_in, gate_bias, sgu_norm, sgu_w, sgu_b, q_norm, k_norm, lambda_q1, lambda_k1, lambda_q2, lambda_k2, diff_subln, w_proj_a, w_proj_b, w_out, ffn2_norm, ffn2_w_in, ffn2_w_out, final_norm):
    depth = ffn1_norm.shape[0]
    y_prompt, y_sample = x_prompt, x_sample
    row = lambda p: p.reshape(1, -1).astype(F32)
    for l in range(depth):
        lambda_init = 0.8 - 0.6 * math.exp(-0.3 * l)
        sw = sgu_w[l].astype(BF16)
        lp = {
            "ffn1_norm": row(ffn1_norm[l]), "ffn1_w_in": ffn1_w_in[l].astype(BF16),
            "ffn1_w_out": ffn1_w_out[l].astype(BF16),
            "mix_norm": row(mix_norm[l]), "w_in": w_in[l].astype(BF16),
            "gate_bias": row(gate_bias[l]), "sgu_norm": row(sgu_norm[l]),
            "wcat": jnp.concatenate([sw[0::2], sw[1::2]], axis=-1),
            "sbias": jnp.repeat(sgu_b[l].T.astype(F32), A_GROUP_DIM, axis=1),
            "seg": jnp.kron(jnp.eye(V7X_MXU_DIM // HEAD_DIM, dtype=F32),
                            jnp.full((HEAD_DIM, HEAD_DIM), 1.0 / HEAD_DIM, F32)).astype(BF16),
            "qg": jnp.tile(row(q_norm[l]) * (HEAD_DIM ** -0.5 * LOG2E), (1, B_QK // HEAD_DIM)),
            "kg": jnp.tile(row(k_norm[l]), (1, B_QK // HEAD_DIM)),
            "diff_subln": row(diff_subln[l]),
            "w_proj_a": w_proj_a[l].astype(BF16), "w_proj_b": w_proj_b[l].astype(BF16),
            "w_out": w_out[l].astype(BF16),
            "ffn2_norm": row(ffn2_norm[l]), "ffn2_w_in": ffn2_w_in[l].astype(BF16),
            "ffn2_w_out": ffn2_w_out[l].astype(BF16), "final_norm": row(final_norm[l]),
        }
        shift, single_pass = _logit_shift(lp["qg"], lp["kg"], rel_bias)
        tiles, tiles_t, lam = _prep(rel_bias.astype(F32), shift, row(lambda_q1[l]), row(lambda_k1[l]),
                                    row(lambda_q2[l]), row(lambda_k2[l]), lambda_init)
        out_scale = 1.0 - lambda_init
        y_prompt = _layer(y_prompt, lp, tiles, tiles_t, lam, out_scale, single_pass)
        y_sample = _layer(y_sample, lp, tiles, tiles_t, lam, out_scale, single_pass)
    return (y_prompt, y_sample)
```

```python
import functools
import math

import jax
import jax.numpy as jnp
from jax import lax
from jax.experimental import pallas as pl
from jax.experimental.pallas import tpu as pltpu

D_MODEL = 1024
D_FF = 2816
CHUNK = 128
A_WIDTH = 512
A_GROUPS = 8
A_GROUP_DIM = A_WIDTH // A_GROUPS
B_HEADS = 4
HEAD_DIM = 64
HEAD_WIDTH = 2 * HEAD_DIM
B_QK = B_HEADS * HEAD_WIDTH
B_V = B_HEADS * HEAD_WIDTH
N_BUCKETS = 32
MAX_DISTANCE = 128
IN_COLS = 2 * A_WIDTH + 2 * B_QK + B_V + 2 * D_MODEL
EPS = 1e-6

V7X_LANES = 128
V7X_MXU_DIM = 256
V7X_VMEM_LIMIT_BYTES = 56 * 1024 * 1024

ROW_TILE = 512
WIDE_ROW_TILE = 1024
SUB_TILE_ROWS = 256
WEIGHT_CHUNKS = 16
ATTN_TILE = 256
ATTN_KEY_STEP = 512
LOG2E = math.log2(math.e)
LOGIT_SPAN = 100.0
N_BIAS_TILES = 5

F32 = jnp.float32
BF16 = jnp.bfloat16


def _rms(x, g):
    return x * lax.rsqrt(jnp.mean(x * x, axis=-1, keepdims=True) + EPS) * g


def _const_spec(shape):
    return pl.BlockSpec(shape, lambda *_: (0,) * len(shape), pipeline_mode=pl.Buffered(1))


def _sequential_params():
    return pltpu.CompilerParams(dimension_semantics=("arbitrary",),
                                vmem_limit_bytes=V7X_VMEM_LIMIT_BYTES)


def _params(n_axes):
    return pltpu.CompilerParams(dimension_semantics=("parallel",) * n_axes,
                                vmem_limit_bytes=V7X_VMEM_LIMIT_BYTES)


def _macaron_half_step(x, g, win_ref, wout_ref):
    h = _rms(x, g).astype(BF16)
    gu = jnp.dot(h, win_ref[...], preferred_element_type=F32)
    gate = gu[:, :D_FF]
    up = gu[:, D_FF:]
    act = (gate * jax.nn.sigmoid(gate) * up).astype(BF16)
    return x + 0.5 * jnp.dot(act, wout_ref[...], preferred_element_type=F32)


def _convert_weight(w_hbm, w_vmem, stage, sem):
    rows = stage.shape[1]
    n_chunks = w_hbm.shape[0] // rows

    def chunk_copy(c):
        return pltpu.make_async_copy(w_hbm.at[pl.ds(c * rows, rows)], stage.at[c % 2], sem.at[c % 2])

    chunk_copy(0).start()
    for c in range(n_chunks):
        if c + 1 < n_chunks:
            chunk_copy(c + 1).start()
        chunk_copy(c).wait()
        w_vmem[c * rows:(c + 1) * rows, :] = stage[c % 2].astype(BF16)


def _ffn_weight_scratch():
    return [pltpu.VMEM((D_MODEL, 2 * D_FF), BF16), pltpu.VMEM((D_FF, D_MODEL), BF16),
            pltpu.VMEM((2, D_MODEL // WEIGHT_CHUNKS, 2 * D_FF), F32),
            pltpu.VMEM((2, D_FF // WEIGHT_CHUNKS, D_MODEL), F32),
            pltpu.SemaphoreType.DMA((2,)), pltpu.SemaphoreType.DMA((2,))]


def _convert_ffn_weights(win_hbm, wout_hbm, win_ref, wout_ref, stage_in, stage_out, sem_in, sem_out):
    @pl.when(pl.program_id(0) == 0)
    def _():
        _convert_weight(win_hbm, win_ref, stage_in, sem_in)
        _convert_weight(wout_hbm, wout_ref, stage_out, sem_out)


def _sub_tiles(n_rows):
    return [slice(r, r + SUB_TILE_ROWS) for r in range(0, n_rows, SUB_TILE_ROWS)]


def _ffn_body(x_ref, g_ref, win_hbm, wout_hbm, o_ref, win_ref, wout_ref, *staging):
    _convert_ffn_weights(win_hbm, wout_hbm, win_ref, wout_ref, *staging)
    for rows in _sub_tiles(x_ref.shape[0]):
        o_ref[rows, :] = _macaron_half_step(x_ref[rows, :], g_ref[...], win_ref, wout_ref)


def _ffn(x, g, w_in, w_out):
    n = x.shape[0]
    row = pl.BlockSpec((WIDE_ROW_TILE, D_MODEL), lambda i: (i, 0))
    return pl.pallas_call(
        _ffn_body,
        grid=(n // WIDE_ROW_TILE,),
        in_specs=[row, _const_spec((1, D_MODEL)), pl.BlockSpec(memory_space=pl.ANY),
                  pl.BlockSpec(memory_space=pl.ANY)],
        out_specs=row,
        out_shape=jax.ShapeDtypeStruct((n, D_MODEL), F32),
        scratch_shapes=_ffn_weight_scratch(),
        compiler_params=_sequential_params(),
        name="ffn",
    )(x, g, w_in, w_out)


def _proj_body(x_ref, *refs):
    n_outs = 5
    params, outs = refs[:-n_outs], refs[-n_outs:]
    for rows in _sub_tiles(x_ref.shape[0]):
        _proj_rows(x_ref.at[rows], *params, *(o.at[rows] for o in outs))


def _proj_rows(x_ref, g_ref, win_ref, gbias_ref, sgug_ref, wcat_ref, sbias_ref, seg_ref,
               qg_ref, kg_ref, a_ref, q_ref, k_ref, v_ref, gate_ref):
    tm = x_ref.shape[0]
    h = _rms(x_ref[...], g_ref[...]).astype(BF16)
    proj = jnp.dot(h, win_ref[...], preferred_element_type=F32)
    c0 = 2 * A_WIDTH
    c1 = c0 + B_QK
    c2 = c1 + B_QK
    c3 = c2 + B_V

    uv = jax.nn.gelu(proj[:, :c0])
    u = uv[:, :A_WIDTH]
    vn = _rms(uv[:, A_WIDTH:], sgug_ref[...])
    lane = lax.broadcasted_iota(jnp.int32, (tm, A_WIDTH), 1)
    low = (lane & (V7X_LANES - 1)) < A_GROUP_DIM
    v_low = jnp.where(low, vn, 0.0).astype(BF16)
    v_high = jnp.where(low, 0.0, vn).astype(BF16)
    for c in range(0, tm // CHUNK, 2):
        pair = [slice((c + i) * CHUNK, (c + i + 1) * CHUNK) for i in range(2)]
        slabs = []
        for p in range(A_WIDTH // V7X_LANES):
            cols = slice(p * V7X_LANES, (p + 1) * V7X_LANES)
            rhs = jnp.concatenate(
                [jnp.concatenate([v_low[rows, cols], v_high[rows, cols]], axis=0)
                 for rows in pair], axis=1)
            slabs.append(jnp.dot(wcat_ref[p], rhs, preferred_element_type=F32))
        for i, rows in enumerate(pair):
            s = jnp.concatenate([slab[:, i * V7X_LANES:(i + 1) * V7X_LANES] for slab in slabs],
                                axis=1) + sbias_ref[...]
            a_ref[rows, :] = (u[rows, :] * s).astype(a_ref.dtype)

    def seg_norm(t, gain):
        sq = (t * t).astype(BF16)
        half = seg_ref.shape[0]
        ms = jnp.concatenate(
            [jnp.dot(sq[:, i * half:(i + 1) * half], seg_ref[...], preferred_element_type=F32)
             for i in range(B_QK // half)], axis=1)
        return (t * lax.rsqrt(ms + EPS) * gain).astype(BF16)

    q_ref[...] = seg_norm(proj[:, c0:c1], qg_ref[...])
    k_ref[...] = seg_norm(proj[:, c1:c2], kg_ref[...])
    v_ref[...] = proj[:, c2:c3].astype(BF16)
    gate_ref[...] = jax.nn.sigmoid(proj[:, c3:] + gbias_ref[...]).astype(gate_ref.dtype)


def _proj(x, g, w_in, gate_bias, sgu_g, wcat, sbias, seg, qg, kg):
    n = x.shape[0]
    row = lambda w: pl.BlockSpec((WIDE_ROW_TILE, w), lambda i: (i, 0))
    out_w = (A_WIDTH, B_QK, B_QK, B_V, 2 * D_MODEL)
    return pl.pallas_call(
        _proj_body,
        grid=(n // WIDE_ROW_TILE,),
        in_specs=[row(D_MODEL), _const_spec((1, D_MODEL)), _const_spec((D_MODEL, IN_COLS)),
                  _const_spec((1, 2 * D_MODEL)), _const_spec((1, A_WIDTH)),
                  _const_spec(wcat.shape), _const_spec(sbias.shape), _const_spec(seg.shape),
                  _const_spec((1, B_QK)), _const_spec((1, B_QK))],
        out_specs=[row(w) for w in out_w],
        out_shape=[jax.ShapeDtypeStruct((n, w), BF16) for w in out_w],
        compiler_params=_params(1),
        name="proj",
    )(x, g, w_in, gate_bias, sgu_g, wcat, sbias, seg, qg, kg)


def _bucket_thresholds():
    half = N_BUCKETS // 2
    max_exact = half // 2
    n_log = half - max_exact
    ratio = MAX_DISTANCE // max_exact
    out = []
    for j in range(1, n_log):
        target = max_exact ** n_log * ratio ** j
        n = max_exact
        while n ** n_log < target:
            n += 1
        out.append(n)
    return max_exact, out


def _prep_body(rb_ref, shift_ref, lq1_ref, lk1_ref, lq2_ref, lk2_ref, tiles_ref, tiles_t_ref,
               lam_ref, *, lambda_init):
    t = ATTN_TILE
    half = N_BUCKETS // 2
    max_exact, thresholds = _bucket_thresholds()
    row = lax.broadcasted_iota(jnp.int32, (t, t), 0)
    col = lax.broadcasted_iota(jnp.int32, (t, t), 1)
    shift = shift_ref[0, 0]
    for hd in range(B_HEADS):
        for idx, b in ((0, half - 1), (N_BIAS_TILES - 1, N_BUCKETS - 1)):
            far = jnp.full((t, t), rb_ref[b, hd] * LOG2E - shift, F32)
            tiles_ref[hd, idx] = far
            tiles_t_ref[hd, idx] = far
    for d in (-1, 0, 1):
        rel = col - row + d * t
        n = jnp.abs(rel)
        large = jnp.full((t, t), max_exact, jnp.int32)
        for th in thresholds:
            large = large + jnp.where(n >= th, 1, 0)
        bucket = jnp.where(rel > 0, half, 0) + jnp.where(n < max_exact, n, large)
        for hd in range(B_HEADS):
            tile = jnp.zeros((t, t), F32)
            for b in range(N_BUCKETS):
                tile = jnp.where(bucket == b, rb_ref[b, hd] * LOG2E - shift, tile)
            tiles_ref[hd, d + N_BIAS_TILES // 2] = tile
            tiles_t_ref[hd, d + N_BIAS_TILES // 2] = tile.T
    s1 = jnp.sum(lq1_ref[...] * lk1_ref[...], axis=-1, keepdims=True)
    s2 = jnp.sum(lq2_ref[...] * lk2_ref[...], axis=-1, keepdims=True)
    lam = jnp.exp(s1) - jnp.exp(s2) + lambda_init
    lam_ref[...] = jnp.broadcast_to(lam, lam_ref.shape)


def _prep(rel_bias, shift, lq1, lk1, lq2, lk2, lambda_init):
    t = ATTN_TILE
    vec = pl.BlockSpec((1, HEAD_DIM), lambda: (0, 0))
    return pl.pallas_call(
        functools.partial(_prep_body, lambda_init=lambda_init),
        in_specs=[pl.BlockSpec(memory_space=pltpu.SMEM), pl.BlockSpec(memory_space=pltpu.SMEM),
                  vec, vec, vec, vec],
        out_specs=[pl.BlockSpec((B_HEADS, N_BIAS_TILES, t, t), lambda: (0, 0, 0, 0)),
                   pl.BlockSpec((B_HEADS, N_BIAS_TILES, t, t), lambda: (0, 0, 0, 0)),
                   pl.BlockSpec((8, V7X_LANES), lambda: (0, 0))],
        out_shape=[jax.ShapeDtypeStruct((B_HEADS, N_BIAS_TILES, t, t), F32),
                   jax.ShapeDtypeStruct((B_HEADS, N_BIAS_TILES, t, t), F32),
                   jax.ShapeDtypeStruct((8, V7X_LANES), F32)],
        name="prep",
    )(rel_bias, shift, lq1, lk1, lq2, lk2)


def _q_stack(q_ref, c):
    t = ATTN_TILE
    lane = lax.broadcasted_iota(jnp.int32, (t, HEAD_WIDTH), 1)
    first = jnp.where(lane < HEAD_DIM, 1.0, 0.0).astype(BF16)
    q = q_ref[pl.ds(pl.multiple_of(c * t, t), t), :]
    return jnp.concatenate([q * first, q * (1.0 - first)], axis=0)


def _biased_logits(q2, c, j, k_ref, tiles_ref):
    t = ATTN_TILE
    ks = ATTN_KEY_STEP
    bias = jnp.concatenate(
        [tiles_ref[jnp.clip(jt - c, -2, 2) + N_BIAS_TILES // 2]
         for jt in range(j * ks // t, (j + 1) * ks // t)], axis=1)
    s = lax.dot_general(q2, k_ref[j * ks:(j + 1) * ks, :], (((1,), (1,)), ((), ())),
                        preferred_element_type=F32)
    return s + jnp.concatenate([bias, bias], axis=0)


def _store_output(res, c, lam_ref, subg_ref, o_ref, out_scale):
    t = ATTN_TILE
    out = res[:t] - lam_ref[0:1, 0:1] * res[t:]
    o_ref[pl.ds(pl.multiple_of(c * t, t), t), :] = (
        _rms(out, subg_ref[...]) * out_scale).astype(o_ref.dtype)


def _attn_exact_body(lam_ref, q_ref, k_ref, v_ref, tiles_ref, subg_ref, o_ref,
                     s_ref, m_ref, mnext_ref, l_ref, acc_ref, *, out_scale):
    t = ATTN_TILE
    ks = ATTN_KEY_STEP
    nk = k_ref.shape[0] // ks
    nq = q_ref.shape[0] // t
    reps = ks // V7X_LANES

    def logits_step(c, q2, j):
        s = _biased_logits(q2, c, j, k_ref, tiles_ref)
        s_ref[:, j * ks:(j + 1) * ks] = s
        blockmax = s[:, :V7X_LANES]
        for r in range(1, reps):
            blockmax = jnp.maximum(blockmax, s[:, r * V7X_LANES:(r + 1) * V7X_LANES])
        if j == 0:
            mnext_ref[...] = blockmax
        else:
            mnext_ref[...] = jnp.maximum(mnext_ref[...], blockmax)

    def finish_logits():
        m = jnp.max(mnext_ref[...], axis=1, keepdims=True)
        m_ref[...] = jnp.broadcast_to(m, m_ref.shape)

    def prob_step(j, s):
        p = jnp.exp2(s - jnp.concatenate([m_ref[...]] * reps, axis=1))
        psum = p[:, :V7X_LANES]
        for r in range(1, reps):
            psum = psum + p[:, r * V7X_LANES:(r + 1) * V7X_LANES]
        pv = jnp.dot(p.astype(BF16), v_ref[j * ks:(j + 1) * ks, :], preferred_element_type=F32)
        if j == 0:
            l_ref[...] = psum
            acc_ref[...] = pv
        else:
            l_ref[...] += psum
            acc_ref[...] += pv

    def finish_block(c):
        l = jnp.sum(l_ref[...], axis=1, keepdims=True)
        _store_output(acc_ref[...] / l, c, lam_ref, subg_ref, o_ref, out_scale)

    q2 = _q_stack(q_ref, 0)
    for j in range(nk):
        logits_step(0, q2, j)
    finish_logits()

    def body(c, carry):
        q2n = _q_stack(q_ref, c + 1)
        for j in range(nk):
            s = s_ref[:, j * ks:(j + 1) * ks]
            logits_step(c + 1, q2n, j)
            prob_step(j, s)
        finish_block(c)
        finish_logits()
        return carry

    lax.fori_loop(0, nq - 1, body, 0)

    for j in range(nk):
        prob_step(j, s_ref[:, j * ks:(j + 1) * ks])
    finish_block(nq - 1)


def _attn_fast_body(lam_ref, q_ref, k_ref, v_ref, tiles_t_ref, subg_ref, o_ref, pt_ref, vt_ref,
                    l_ref, *, out_scale):
    t = ATTN_TILE
    ks = ATTN_KEY_STEP
    seq = k_ref.shape[0]
    nk = seq // ks
    nq = q_ref.shape[0] // t

    vt_ref[...] = v_ref[...].astype(F32).T.astype(BF16)

    def prob_block(c):
        q2 = _q_stack(q_ref, c)
        total = None
        for j in range(nk):
            keys = slice(j * ks, (j + 1) * ks)
            bias = jnp.concatenate(
                [tiles_t_ref[jnp.clip(jt - c, -2, 2) + N_BIAS_TILES // 2]
                 for jt in range(j * ks // t, (j + 1) * ks // t)], axis=0)
            s = lax.dot_general(k_ref[keys, :], q2, (((1,), (1,)), ((), ())),
                                preferred_element_type=F32)
            p = jnp.exp2(s + jnp.concatenate([bias, bias], axis=1))
            part = jnp.sum(p, axis=0, keepdims=True)
            total = part if total is None else total + part
            pt_ref[keys, :] = p.astype(BF16)
        l_ref[...] = jnp.broadcast_to(total, l_ref.shape)

    def output_block(c):
        res = jnp.dot(vt_ref[...], pt_ref[...], preferred_element_type=F32)
        res = res / l_ref[0:1, :]
        out = res[:, :t] - lam_ref[0:1, 0:1] * res[:, t:]
        out = out * lax.rsqrt(jnp.mean(out * out, axis=0, keepdims=True) + EPS)
        o_ref[pl.ds(pl.multiple_of(c * t, t), t), :] = (
            out.T * subg_ref[...] * out_scale).astype(o_ref.dtype)

    prob_block(0)

    def body(c, carry):
        output_block(c - 1)
        prob_block(c)
        return carry

    lax.fori_loop(1, nq, body, 0, unroll=4)
    output_block(nq - 1)


def _attn(lam, q, k, v, tiles, tiles_t, subg, *, out_scale, single_pass):
    b, s, _ = q.shape
    t = ATTN_TILE
    seq = pl.BlockSpec((None, s, HEAD_WIDTH), lambda bi, hi: (bi, 0, hi))
    stack = pltpu.VMEM((2 * t, V7X_LANES), F32)
    if single_pass:
        body, name, bias_tiles = _attn_fast_body, "attn_fast", tiles_t
        scratch = [pltpu.VMEM((s, 2 * t), BF16), pltpu.VMEM((HEAD_WIDTH, s), BF16),
                   pltpu.VMEM((8, 2 * t), F32)]
    else:
        body, name, bias_tiles = _attn_exact_body, "attn_exact", tiles
        scratch = [pltpu.VMEM((2 * t, s), F32), stack, stack, stack, stack]
    return pl.pallas_call(
        functools.partial(body, out_scale=out_scale),
        grid=(b, B_HEADS),
        in_specs=[pl.BlockSpec((8, V7X_LANES), lambda bi, hi: (0, 0)),
                  seq, seq, seq,
                  pl.BlockSpec((None, N_BIAS_TILES, t, t), lambda bi, hi: (hi, 0, 0, 0)),
                  pl.BlockSpec((1, HEAD_WIDTH), lambda bi, hi: (0, 0))],
        out_specs=seq,
        out_shape=jax.ShapeDtypeStruct((b, s, B_V), BF16),
        scratch_shapes=scratch,
        compiler_params=_params(2),
        name=name,
    )(lam, q, k, v, bias_tiles, subg)


def _logit_shift(q_gain, k_gain, rel_bias):
    qk_bound = 1.02 * HEAD_DIM * jnp.max(jnp.abs(q_gain)) * jnp.max(jnp.abs(k_gain))
    bias = rel_bias.astype(F32) * LOG2E
    shift = qk_bound + jnp.max(bias)
    span = 2.0 * qk_bound + jnp.max(bias) - jnp.min(bias)
    return shift.reshape(1, 1), span <= LOGIT_SPAN


def _merge_ffn_body(x_ref, a_ref, b_ref, gate_ref, wpa_ref, wpb_ref, wo_ref, g_ref, win_hbm,
                    wout_hbm, fg_ref, o_ref, win_ref, wout_ref, *staging):
    _convert_ffn_weights(win_hbm, wout_hbm, win_ref, wout_ref, *staging)
    ga = gate_ref[:, :D_MODEL].astype(F32)
    gb = gate_ref[:, D_MODEL:].astype(F32)
    merged = (ga * jnp.dot(a_ref[...], wpa_ref[...], preferred_element_type=F32)
              + gb * jnp.dot(b_ref[...], wpb_ref[...], preferred_element_type=F32))
    x = x_ref[...] + jnp.dot(merged.astype(BF16), wo_ref[...], preferred_element_type=F32)
    for rows in _sub_tiles(x.shape[0]):
        o_ref[rows, :] = _rms(_macaron_half_step(x[rows, :], g_ref[...], win_ref, wout_ref),
                              fg_ref[...])


def _merge_ffn(x, a, b, gates, wpa, wpb, wo, g, w_in, w_out, final_g):
    n = x.shape[0]
    row = lambda w: pl.BlockSpec((ROW_TILE, w), lambda i: (i, 0))
    return pl.pallas_call(
        _merge_ffn_body,
        grid=(n // ROW_TILE,),
        in_specs=[row(D_MODEL), row(A_WIDTH), row(B_V), row(2 * D_MODEL),
                  _const_spec((A_WIDTH, D_MODEL)), _const_spec((B_V, D_MODEL)),
                  _const_spec((D_MODEL, D_MODEL)), _const_spec((1, D_MODEL)),
                  pl.BlockSpec(memory_space=pl.ANY), pl.BlockSpec(memory_space=pl.ANY),
                  _const_spec((1, D_MODEL))],
        out_specs=row(D_MODEL),
        out_shape=jax.ShapeDtypeStruct((n, D_MODEL), F32),
        scratch_shapes=_ffn_weight_scratch(),
        compiler_params=_sequential_params(),
        name="merge_ffn",
    )(x, a, b, gates, wpa, wpb, wo, g, w_in, w_out, final_g)


def _layer(x, lp, tiles, tiles_t, lam, out_scale, single_pass):
    b, s, _ = x.shape
    x0 = x.reshape(b * s, D_MODEL)
    x1 = _ffn(x0, lp["ffn1_norm"], lp["ffn1_w_in"], lp["ffn1_w_out"])
    a, q, k, v, gates = _proj(x1, lp["mix_norm"], lp["w_in"], lp["gate_bias"], lp["sgu_norm"],
                              lp["wcat"], lp["sbias"], lp["seg"], lp["qg"], lp["kg"])
    bo = lax.cond(single_pass,
                  functools.partial(_attn, out_scale=out_scale, single_pass=True),
                  functools.partial(_attn, out_scale=out_scale, single_pass=False),
                  lam, q.reshape(b, s, B_QK), k.reshape(b, s, B_QK), v.reshape(b, s, B_V),
                  tiles, tiles_t, lp["diff_subln"])
    y = _merge_ffn(x1, a, bo.reshape(b * s, B_V), gates, lp["w_proj_a"], lp["w_proj_b"], lp["w_out"],
                   lp["ffn2_norm"], lp["ffn2_w_in"], lp["ffn2_w_out"], lp["final_norm"])
    return y.reshape(b, s, D_MODEL)


def kernel(x_prompt, x_sample, rel_bias, ffn1_norm, ffn1_w_in, ffn1_w_out, mix_norm, w_in, gate_bias, sgu_norm, sgu_w, sgu_b, q_norm, k_norm, lambda_q1, lambda_k1, lambda_q2, lambda_k2, diff_subln, w_proj_a, w_proj_b, w_out, ffn2_norm, ffn2_w_in, ffn2_w_out, final_norm):
    depth = ffn1_norm.shape[0]
    y_prompt, y_sample = x_prompt, x_sample
    row = lambda p: p.reshape(1, -1).astype(F32)
    for l in range(depth):
        lambda_init = 0.8 - 0.6 * math.exp(-0.3 * l)
        sw = sgu_w[l].astype(BF16)
        lp = {
            "ffn1_norm": row(ffn1_norm[l]), "ffn1_w_in": ffn1_w_in[l].astype(F32),
            "ffn1_w_out": ffn1_w_out[l].astype(F32),
            "mix_norm": row(mix_norm[l]), "w_in": w_in[l].astype(BF16),
            "gate_bias": row(gate_bias[l]), "sgu_norm": row(sgu_norm[l]),
            "wcat": jnp.concatenate([sw[0::2], sw[1::2]], axis=-1),
            "sbias": jnp.repeat(sgu_b[l].T.astype(F32), A_GROUP_DIM, axis=1),
            "seg": jnp.kron(jnp.eye(V7X_MXU_DIM // HEAD_DIM, dtype=F32),
                            jnp.full((HEAD_DIM, HEAD_DIM), 1.0 / HEAD_DIM, F32)).astype(BF16),
            "qg": jnp.tile(row(q_norm[l]) * (HEAD_DIM ** -0.5 * LOG2E), (1, B_QK // HEAD_DIM)),
            "kg": jnp.tile(row(k_norm[l]), (1, B_QK // HEAD_DIM)),
            "diff_subln": row(diff_subln[l]),
            "w_proj_a": w_proj_a[l].astype(BF16), "w_proj_b": w_proj_b[l].astype(BF16),
            "w_out": w_out[l].astype(BF16),
            "ffn2_norm": row(ffn2_norm[l]), "ffn2_w_in": ffn2_w_in[l].astype(F32),
            "ffn2_w_out": ffn2_w_out[l].astype(F32), "final_norm": row(final_norm[l]),
        }
        shift, single_pass = _logit_shift(lp["qg"], lp["kg"], rel_bias)
        tiles, tiles_t, lam = _prep(rel_bias.astype(F32), shift, row(lambda_q1[l]), row(lambda_k1[l]),
                                    row(lambda_q2[l]), row(lambda_k2[l]), lambda_init)
        out_scale = 1.0 - lambda_init
        y_prompt = _layer(y_prompt, lp, tiles, tiles_t, lam, out_scale, single_pass)
        y_sample = _layer(y_sample, lp, tiles, tiles_t, lam, out_scale, single_pass)
    return (y_prompt, y_sample)
```

```python
import functools
import math

import jax
import jax.numpy as jnp
from jax import lax
from jax.experimental import pallas as pl
from jax.experimental.pallas import tpu as pltpu

D_MODEL = 1024
D_FF = 2816
CHUNK = 128
A_WIDTH = 512
A_GROUPS = 8
A_GROUP_DIM = A_WIDTH // A_GROUPS
B_HEADS = 4
HEAD_DIM = 64
HEAD_WIDTH = 2 * HEAD_DIM
B_QK = B_HEADS * HEAD_WIDTH
B_V = B_HEADS * HEAD_WIDTH
N_BUCKETS = 32
MAX_DISTANCE = 128
IN_COLS = 2 * A_WIDTH + 2 * B_QK + B_V + 2 * D_MODEL
EPS = 1e-6

V7X_LANES = 128
V7X_MXU_DIM = 256
V7X_VMEM_LIMIT_BYTES = 56 * 1024 * 1024

ROW_TILE = 512
WIDE_ROW_TILE = 1024
SUB_TILE_ROWS = 256
ATTN_TILE = 256
ATTN_KEY_STEP = 512
LOG2E = math.log2(math.e)
LOGIT_SPAN = 100.0
N_BIAS_TILES = 5

F32 = jnp.float32
BF16 = jnp.bfloat16


def _rms(x, g):
    return x * lax.rsqrt(jnp.mean(x * x, axis=-1, keepdims=True) + EPS) * g


def _const_spec(shape):
    return pl.BlockSpec(shape, lambda *_: (0,) * len(shape), pipeline_mode=pl.Buffered(1))


def _params(n_axes):
    return pltpu.CompilerParams(dimension_semantics=("parallel",) * n_axes,
                                vmem_limit_bytes=V7X_VMEM_LIMIT_BYTES)


def _macaron_half_step(x, g, win_ref, wout_ref):
    h = _rms(x, g).astype(BF16)
    gu = jnp.dot(h, win_ref[...], preferred_element_type=F32)
    gate = gu[:, :D_FF]
    up = gu[:, D_FF:]
    act = (gate * jax.nn.sigmoid(gate) * up).astype(BF16)
    return x + 0.5 * jnp.dot(act, wout_ref[...], preferred_element_type=F32)


def _sub_tiles(n_rows):
    return [slice(r, r + SUB_TILE_ROWS) for r in range(0, n_rows, SUB_TILE_ROWS)]


def _ffn_body(x_ref, g_ref, win_ref, wout_ref, o_ref):
    for rows in _sub_tiles(x_ref.shape[0]):
        o_ref[rows, :] = _macaron_half_step(x_ref[rows, :], g_ref[...], win_ref, wout_ref)


def _ffn(x, g, w_in, w_out):
    n = x.shape[0]
    row = pl.BlockSpec((WIDE_ROW_TILE, D_MODEL), lambda i: (i, 0))
    return pl.pallas_call(
        _ffn_body,
        grid=(n // WIDE_ROW_TILE,),
        in_specs=[row, _const_spec((1, D_MODEL)), _const_spec((D_MODEL, 2 * D_FF)),
                  _const_spec((D_FF, D_MODEL))],
        out_specs=row,
        out_shape=jax.ShapeDtypeStruct((n, D_MODEL), F32),
        compiler_params=_params(1),
        name="ffn",
    )(x, g, w_in, w_out)


def _proj_body(x_ref, *refs):
    n_outs = 5
    params, outs = refs[:-n_outs], refs[-n_outs:]
    for rows in _sub_tiles(x_ref.shape[0]):
        _proj_rows(x_ref.at[rows], *params, *(o.at[rows] for o in outs))


def _proj_rows(x_ref, g_ref, win_ref, gbias_ref, sgug_ref, wcat_ref, sbias_ref, seg_ref,
               qg_ref, kg_ref, a_ref, q_ref, k_ref, v_ref, gate_ref):
    tm = x_ref.shape[0]
    h = _rms(x_ref[...], g_ref[...]).astype(BF16)
    proj = jnp.dot(h, win_ref[...], preferred_element_type=F32)
    c0 = 2 * A_WIDTH
    c1 = c0 + B_QK
    c2 = c1 + B_QK
    c3 = c2 + B_V

    uv = jax.nn.gelu(proj[:, :c0])
    u = uv[:, :A_WIDTH]
    vn = _rms(uv[:, A_WIDTH:], sgug_ref[...])
    lane = lax.broadcasted_iota(jnp.int32, (tm, A_WIDTH), 1)
    low = (lane & (V7X_LANES - 1)) < A_GROUP_DIM
    v_low = jnp.where(low, vn, 0.0).astype(BF16)
    v_high = jnp.where(low, 0.0, vn).astype(BF16)
    for c in range(0, tm // CHUNK, 2):
        pair = [slice((c + i) * CHUNK, (c + i + 1) * CHUNK) for i in range(2)]
        slabs = []
        for p in range(A_WIDTH // V7X_LANES):
            cols = slice(p * V7X_LANES, (p + 1) * V7X_LANES)
            rhs = jnp.concatenate(
                [jnp.concatenate([v_low[rows, cols], v_high[rows, cols]], axis=0)
                 for rows in pair], axis=1)
            slabs.append(jnp.dot(wcat_ref[p], rhs, preferred_element_type=F32))
        for i, rows in enumerate(pair):
            s = jnp.concatenate([slab[:, i * V7X_LANES:(i + 1) * V7X_LANES] for slab in slabs],
                                axis=1) + sbias_ref[...]
            a_ref[rows, :] = (u[rows, :] * s).astype(a_ref.dtype)

    def seg_norm(t, gain):
        sq = (t * t).astype(BF16)
        half = seg_ref.shape[0]
        ms = jnp.concatenate(
            [jnp.dot(sq[:, i * half:(i + 1) * half], seg_ref[...], preferred_element_type=F32)
             for i in range(B_QK // half)], axis=1)
        return (t * lax.rsqrt(ms + EPS) * gain).astype(BF16)

    q_ref[...] = seg_norm(proj[:, c0:c1], qg_ref[...])
    k_ref[...] = seg_norm(proj[:, c1:c2], kg_ref[...])
    v_ref[...] = proj[:, c2:c3].astype(BF16)
    gate_ref[...] = jax.nn.sigmoid(proj[:, c3:] + gbias_ref[...]).astype(gate_ref.dtype)


def _proj(x, g, w_in, gate_bias, sgu_g, wcat, sbias, seg, qg, kg):
    n = x.shape[0]
    row = lambda w: pl.BlockSpec((WIDE_ROW_TILE, w), lambda i: (i, 0))
    out_w = (A_WIDTH, B_QK, B_QK, B_V, 2 * D_MODEL)
    return pl.pallas_call(
        _proj_body,
        grid=(n // WIDE_ROW_TILE,),
        in_specs=[row(D_MODEL), _const_spec((1, D_MODEL)), _const_spec((D_MODEL, IN_COLS)),
                  _const_spec((1, 2 * D_MODEL)), _const_spec((1, A_WIDTH)),
                  _const_spec(wcat.shape), _const_spec(sbias.shape), _const_spec(seg.shape),
                  _const_spec((1, B_QK)), _const_spec((1, B_QK))],
        out_specs=[row(w) for w in out_w],
        out_shape=[jax.ShapeDtypeStruct((n, w), BF16) for w in out_w],
        compiler_params=_params(1),
        name="proj",
    )(x, g, w_in, gate_bias, sgu_g, wcat, sbias, seg, qg, kg)


def _bucket_thresholds():
    half = N_BUCKETS // 2
    max_exact = half // 2
    n_log = half - max_exact
    ratio = MAX_DISTANCE // max_exact
    out = []
    for j in range(1, n_log):
        target = max_exact ** n_log * ratio ** j
        n = max_exact
        while n ** n_log < target:
            n += 1
        out.append(n)
    return max_exact, out


def _prep_body(rb_ref, shift_ref, lq1_ref, lk1_ref, lq2_ref, lk2_ref, tiles_ref, tiles_t_ref,
               lam_ref, *, lambda_init):
    t = ATTN_TILE
    half = N_BUCKETS // 2
    max_exact, thresholds = _bucket_thresholds()
    row = lax.broadcasted_iota(jnp.int32, (t, t), 0)
    col = lax.broadcasted_iota(jnp.int32, (t, t), 1)
    shift = shift_ref[0, 0]
    for hd in range(B_HEADS):
        for idx, b in ((0, half - 1), (N_BIAS_TILES - 1, N_BUCKETS - 1)):
            far = jnp.full((t, t), rb_ref[b, hd] * LOG2E - shift, F32)
            tiles_ref[hd, idx] = far
            tiles_t_ref[hd, idx] = far
    for d in (-1, 0, 1):
        rel = col - row + d * t
        n = jnp.abs(rel)
        large = jnp.full((t, t), max_exact, jnp.int32)
        for th in thresholds:
            large = large + jnp.where(n >= th, 1, 0)
        bucket = jnp.where(rel > 0, half, 0) + jnp.where(n < max_exact, n, large)
        for hd in range(B_HEADS):
            tile = jnp.zeros((t, t), F32)
            for b in range(N_BUCKETS):
                tile = jnp.where(bucket == b, rb_ref[b, hd] * LOG2E - shift, tile)
            tiles_ref[hd, d + N_BIAS_TILES // 2] = tile
            tiles_t_ref[hd, d + N_BIAS_TILES // 2] = tile.T
    s1 = jnp.sum(lq1_ref[...] * lk1_ref[...], axis=-1, keepdims=True)
    s2 = jnp.sum(lq2_ref[...] * lk2_ref[...], axis=-1, keepdims=True)
    lam = jnp.exp(s1) - jnp.exp(s2) + lambda_init
    lam_ref[...] = jnp.broadcast_to(lam, lam_ref.shape)


def _prep(rel_bias, shift, lq1, lk1, lq2, lk2, lambda_init):
    t = ATTN_TILE
    vec = pl.BlockSpec((1, HEAD_DIM), lambda: (0, 0))
    return pl.pallas_call(
        functools.partial(_prep_body, lambda_init=lambda_init),
        in_specs=[pl.BlockSpec(memory_space=pltpu.SMEM), pl.BlockSpec(memory_space=pltpu.SMEM),
                  vec, vec, vec, vec],
        out_specs=[pl.BlockSpec((B_HEADS, N_BIAS_TILES, t, t), lambda: (0, 0, 0, 0)),
                   pl.BlockSpec((B_HEADS, N_BIAS_TILES, t, t), lambda: (0, 0, 0, 0)),
                   pl.BlockSpec((8, V7X_LANES), lambda: (0, 0))],
        out_shape=[jax.ShapeDtypeStruct((B_HEADS, N_BIAS_TILES, t, t), F32),
                   jax.ShapeDtypeStruct((B_HEADS, N_BIAS_TILES, t, t), F32),
                   jax.ShapeDtypeStruct((8, V7X_LANES), F32)],
        name="prep",
    )(rel_bias, shift, lq1, lk1, lq2, lk2)


def _q_stack(q_ref, c):
    t = ATTN_TILE
    lane = lax.broadcasted_iota(jnp.int32, (t, HEAD_WIDTH), 1)
    first = jnp.where(lane < HEAD_DIM, 1.0, 0.0).astype(BF16)
    q = q_ref[pl.ds(pl.multiple_of(c * t, t), t), :]
    return jnp.concatenate([q * first, q * (1.0 - first)], axis=0)


def _biased_logits(q2, c, j, k_ref, tiles_ref):
    t = ATTN_TILE
    ks = ATTN_KEY_STEP
    bias = jnp.concatenate(
        [tiles_ref[jnp.clip(jt - c, -2, 2) + N_BIAS_TILES // 2]
         for jt in range(j * ks // t, (j + 1) * ks // t)], axis=1)
    s = lax.dot_general(q2, k_ref[j * ks:(j + 1) * ks, :], (((1,), (1,)), ((), ())),
                        preferred_element_type=F32)
    return s + jnp.concatenate([bias, bias], axis=0)


def _store_output(res, c, lam_ref, subg_ref, o_ref, out_scale):
    t = ATTN_TILE
    out = res[:t] - lam_ref[0:1, 0:1] * res[t:]
    o_ref[pl.ds(pl.multiple_of(c * t, t), t), :] = (
        _rms(out, subg_ref[...]) * out_scale).astype(o_ref.dtype)


def _attn_exact_body(lam_ref, q_ref, k_ref, v_ref, tiles_ref, subg_ref, o_ref,
                     s_ref, m_ref, mnext_ref, l_ref, acc_ref, *, out_scale):
    t = ATTN_TILE
    ks = ATTN_KEY_STEP
    nk = k_ref.shape[0] // ks
    nq = q_ref.shape[0] // t
    reps = ks // V7X_LANES

    def logits_step(c, q2, j):
        s = _biased_logits(q2, c, j, k_ref, tiles_ref)
        s_ref[:, j * ks:(j + 1) * ks] = s
        blockmax = s[:, :V7X_LANES]
        for r in range(1, reps):
            blockmax = jnp.maximum(blockmax, s[:, r * V7X_LANES:(r + 1) * V7X_LANES])
        if j == 0:
            mnext_ref[...] = blockmax
        else:
            mnext_ref[...] = jnp.maximum(mnext_ref[...], blockmax)

    def finish_logits():
        m = jnp.max(mnext_ref[...], axis=1, keepdims=True)
        m_ref[...] = jnp.broadcast_to(m, m_ref.shape)

    def prob_step(j, s):
        p = jnp.exp2(s - jnp.concatenate([m_ref[...]] * reps, axis=1))
        psum = p[:, :V7X_LANES]
        for r in range(1, reps):
            psum = psum + p[:, r * V7X_LANES:(r + 1) * V7X_LANES]
        pv = jnp.dot(p.astype(BF16), v_ref[j * ks:(j + 1) * ks, :], preferred_element_type=F32)
        if j == 0:
            l_ref[...] = psum
            acc_ref[...] = pv
        else:
            l_ref[...] += psum
            acc_ref[...] += pv

    def finish_block(c):
        l = jnp.sum(l_ref[...], axis=1, keepdims=True)
        _store_output(acc_ref[...] / l, c, lam_ref, subg_ref, o_ref, out_scale)

    q2 = _q_stack(q_ref, 0)
    for j in range(nk):
        logits_step(0, q2, j)
    finish_logits()

    def body(c, carry):
        q2n = _q_stack(q_ref, c + 1)
        for j in range(nk):
            s = s_ref[:, j * ks:(j + 1) * ks]
            logits_step(c + 1, q2n, j)
            prob_step(j, s)
        finish_block(c)
        finish_logits()
        return carry

    lax.fori_loop(0, nq - 1, body, 0)

    for j in range(nk):
        prob_step(j, s_ref[:, j * ks:(j + 1) * ks])
    finish_block(nq - 1)


def _attn_fast_body(lam_ref, q_ref, k_ref, v_ref, tiles_t_ref, subg_ref, o_ref, pt_ref, vt_ref,
                    l_ref, *, out_scale):
    t = ATTN_TILE
    ks = ATTN_KEY_STEP
    seq = k_ref.shape[0]
    nk = seq // ks
    nq = q_ref.shape[0] // t

    vt_ref[...] = v_ref[...].astype(F32).T.astype(BF16)

    def prob_block(c):
        q2 = _q_stack(q_ref, c)
        total = None
        for j in range(nk):
            keys = slice(j * ks, (j + 1) * ks)
            bias = jnp.concatenate(
                [tiles_t_ref[jnp.clip(jt - c, -2, 2) + N_BIAS_TILES // 2]
                 for jt in range(j * ks // t, (j + 1) * ks // t)], axis=0)
            s = lax.dot_general(k_ref[keys, :], q2, (((1,), (1,)), ((), ())),
                                preferred_element_type=F32)
            p = jnp.exp2(s + jnp.concatenate([bias, bias], axis=1))
            part = jnp.sum(p, axis=0, keepdims=True)
            total = part if total is None else total + part
            pt_ref[keys, :] = p.astype(BF16)
        l_ref[...] = jnp.broadcast_to(total, l_ref.shape)

    def output_block(c):
        res = jnp.dot(vt_ref[...], pt_ref[...], preferred_element_type=F32)
        res = res / l_ref[0:1, :]
        out = res[:, :t] - lam_ref[0:1, 0:1] * res[:, t:]
        out = out * lax.rsqrt(jnp.mean(out * out, axis=0, keepdims=True) + EPS)
        o_ref[pl.ds(pl.multiple_of(c * t, t), t), :] = (
            out.T * subg_ref[...] * out_scale).astype(o_ref.dtype)

    prob_block(0)

    def body(c, carry):
        output_block(c - 1)
        prob_block(c)
        return carry

    lax.fori_loop(1, nq, body, 0, unroll=4)
    output_block(nq - 1)


def _attn(lam, q, k, v, tiles, tiles_t, subg, *, out_scale, single_pass):
    b, s, _ = q.shape
    t = ATTN_TILE
    seq = pl.BlockSpec((None, s, HEAD_WIDTH), lambda bi, hi: (bi, 0, hi))
    stack = pltpu.VMEM((2 * t, V7X_LANES), F32)
    if single_pass:
        body, name, bias_tiles = _attn_fast_body, "attn_fast", tiles_t
        scratch = [pltpu.VMEM((s, 2 * t), BF16), pltpu.VMEM((HEAD_WIDTH, s), BF16),
                   pltpu.VMEM((8, 2 * t), F32)]
    else:
        body, name, bias_tiles = _attn_exact_body, "attn_exact", tiles
        scratch = [pltpu.VMEM((2 * t, s), F32), stack, stack, stack, stack]
    return pl.pallas_call(
        functools.partial(body, out_scale=out_scale),
        grid=(b, B_HEADS),
        in_specs=[pl.BlockSpec((8, V7X_LANES), lambda bi, hi: (0, 0)),
                  seq, seq, seq,
                  pl.BlockSpec((None, N_BIAS_TILES, t, t), lambda bi, hi: (hi, 0, 0, 0)),
                  pl.BlockSpec((1, HEAD_WIDTH), lambda bi, hi: (0, 0))],
        out_specs=seq,
        out_shape=jax.ShapeDtypeStruct((b, s, B_V), BF16),
        scratch_shapes=scratch,
        compiler_params=_params(2),
        name=name,
    )(lam, q, k, v, bias_tiles, subg)


def _logit_shift(q_gain, k_gain, rel_bias):
    qk_bound = 1.02 * HEAD_DIM * jnp.max(jnp.abs(q_gain)) * jnp.max(jnp.abs(k_gain))
    bias = rel_bias.astype(F32) * LOG2E
    shift = qk_bound + jnp.max(bias)
    span = 2.0 * qk_bound + jnp.max(bias) - jnp.min(bias)
    return shift.reshape(1, 1), span <= LOGIT_SPAN


def _merge_ffn_body(x_ref, a_ref, b_ref, gate_ref, wpa_ref, wpb_ref, wo_ref, g_ref, win_ref,
                    wout_ref, fg_ref, o_ref):
    for start in range(0, x_ref.shape[0], ROW_TILE):
        tile = slice(start, start + ROW_TILE)
        ga = gate_ref[tile, :D_MODEL].astype(F32)
        gb = gate_ref[tile, D_MODEL:].astype(F32)
        merged = (ga * jnp.dot(a_ref[tile, :], wpa_ref[...], preferred_element_type=F32)
                  + gb * jnp.dot(b_ref[tile, :], wpb_ref[...], preferred_element_type=F32))
        x = x_ref[tile, :] + jnp.dot(merged.astype(BF16), wo_ref[...],
                                     preferred_element_type=F32)
        for rows in _sub_tiles(ROW_TILE):
            out_rows = slice(start + rows.start, start + rows.stop)
            o_ref[out_rows, :] = _rms(
                _macaron_half_step(x[rows, :], g_ref[...], win_ref, wout_ref), fg_ref[...])


def _merge_ffn(x, a, b, gates, wpa, wpb, wo, g, w_in, w_out, final_g):
    n = x.shape[0]
    row = lambda w: pl.BlockSpec((WIDE_ROW_TILE, w), lambda i: (i, 0))
    return pl.pallas_call(
        _merge_ffn_body,
        grid=(n // WIDE_ROW_TILE,),
        in_specs=[row(D_MODEL), row(A_WIDTH), row(B_V), row(2 * D_MODEL),
                  _const_spec((A_WIDTH, D_MODEL)), _const_spec((B_V, D_MODEL)),
                  _const_spec((D_MODEL, D_MODEL)), _const_spec((1, D_MODEL)),
                  _const_spec((D_MODEL, 2 * D_FF)), _const_spec((D_FF, D_MODEL)),
                  _const_spec((1, D_MODEL))],
        out_specs=row(D_MODEL),
        out_shape=jax.ShapeDtypeStruct((n, D_MODEL), F32),
        compiler_params=_params(1),
        name="merge_ffn",
    )(x, a, b, gates, wpa, wpb, wo, g, w_in, w_out, final_g)


def _layer(x, lp, tiles, tiles_t, lam, out_scale, single_pass):
    b, s, _ = x.shape
    x0 = x.reshape(b * s, D_MODEL)
    x1 = _ffn(x0, lp["ffn1_norm"], lp["ffn1_w_in"], lp["ffn1_w_out"])
    a, q, k, v, gates = _proj(x1, lp["mix_norm"], lp["w_in"], lp["gate_bias"], lp["sgu_norm"],
                              lp["wcat"], lp["sbias"], lp["seg"], lp["qg"], lp["kg"])
    bo = lax.cond(single_pass,
                  functools.partial(_attn, out_scale=out_scale, single_pass=True),
                  functools.partial(_attn, out_scale=out_scale, single_pass=False),
                  lam, q.reshape(b, s, B_QK), k.reshape(b, s, B_QK), v.reshape(b, s, B_V),
                  tiles, tiles_t, lp["diff_subln"])
    y = _merge_ffn(x1, a, bo.reshape(b * s, B_V), gates, lp["w_proj_a"], lp["w_proj_b"], lp["w_out"],
                   lp["ffn2_norm"], lp["ffn2_w_in"], lp["ffn2_w_out"], lp["final_norm"])
    return y.reshape(b, s, D_MODEL)


def kernel(x_prompt, x_sample, rel_bias, ffn1_norm, ffn1_w_in, ffn1_w_out, mix_norm, w_in, gate_bias, sgu_norm, sgu_w, sgu_b, q_norm, k_norm, lambda_q1, lambda_k1, lambda_q2, lambda_k2, diff_subln, w_proj_a, w_proj_b, w_out, ffn2_norm, ffn2_w_in, ffn2_w_out, final_norm):
    depth = ffn1_norm.shape[0]
    y_prompt, y_sample = x_prompt, x_sample
    row = lambda p: p.reshape(1, -1).astype(F32)
    for l in range(depth):
        lambda_init = 0.8 - 0.6 * math.exp(-0.3 * l)
        sw = sgu_w[l].astype(BF16)
        lp = {
            "ffn1_norm": row(ffn1_norm[l]), "ffn1_w_in": ffn1_w_in[l].astype(BF16),
            "ffn1_w_out": ffn1_w_out[l].astype(BF16),
            "mix_norm": row(mix_norm[l]), "w_in": w_in[l].astype(BF16),
            "gate_bias": row(gate_bias[l]), "sgu_norm": row(sgu_norm[l]),
            "wcat": jnp.concatenate([sw[0::2], sw[1::2]], axis=-1),
            "sbias": jnp.repeat(sgu_b[l].T.astype(F32), A_GROUP_DIM, axis=1),
            "seg": jnp.kron(jnp.eye(V7X_MXU_DIM // HEAD_DIM, dtype=F32),
                            jnp.full((HEAD_DIM, HEAD_DIM), 1.0 / HEAD_DIM, F32)).astype(BF16),
            "qg": jnp.tile(row(q_norm[l]) * (HEAD_DIM ** -0.5 * LOG2E), (1, B_QK // HEAD_DIM)),
            "kg": jnp.tile(row(k_norm[l]), (1, B_QK // HEAD_DIM)),
            "diff_subln": row(diff_subln[l]),
            "w_proj_a": w_proj_a[l].astype(BF16), "w_proj_b": w_proj_b[l].astype(BF16),
            "w_out": w_out[l].astype(BF16),
            "ffn2_norm": row(ffn2_norm[l]), "ffn2_w_in": ffn2_w_in[l].astype(BF16),
            "ffn2_w_out": ffn2_w_out[l].astype(BF16), "final_norm": row(final_norm[l]),
        }
        shift, single_pass = _logit_shift(lp["qg"], lp["kg"], rel_bias)
        tiles, tiles_t, lam = _prep(rel_bias.astype(F32), shift, row(lambda_q1[l]), row(lambda_k1[l]),
                                    row(lambda_q2[l]), row(lambda_k2[l]), lambda_init)
        out_scale = 1.0 - lambda_init
        y_prompt = _layer(y_prompt, lp, tiles, tiles_t, lam, out_scale, single_pass)
        y_sample = _layer(y_sample, lp, tiles, tiles_t, lam, out_scale, single_pass)
    return (y_prompt, y_sample)
```

```python
import functools
import math

import jax
import jax.numpy as jnp
from jax import lax
from jax.experimental import pallas as pl
from jax.experimental.pallas import tpu as pltpu

D_MODEL = 1024
D_FF = 2816
CHUNK = 128
A_WIDTH = 512
A_GROUPS = 8
A_GROUP_DIM = A_WIDTH // A_GROUPS
B_HEADS = 4
HEAD_DIM = 64
HEAD_WIDTH = 2 * HEAD_DIM
B_QK = B_HEADS * HEAD_WIDTH
B_V = B_HEADS * HEAD_WIDTH
N_BUCKETS = 32
MAX_DISTANCE = 128
IN_COLS = 2 * A_WIDTH + 2 * B_QK + B_V + 2 * D_MODEL
EPS = 1e-6

V7X_LANES = 128
V7X_MXU_DIM = 256
V7X_VMEM_LIMIT_BYTES = 56 * 1024 * 1024

ROW_TILE = 512
WIDE_ROW_TILE = 1024
SUB_TILE_ROWS = 256
ATTN_TILE = 256
ATTN_KEY_STEP = 512
LOG2E = math.log2(math.e)
LOGIT_SPAN = 100.0
N_BIAS_TILES = 5

F32 = jnp.float32
BF16 = jnp.bfloat16


def _rms(x, g):
    return x * lax.rsqrt(jnp.mean(x * x, axis=-1, keepdims=True) + EPS) * g


def _const_spec(shape):
    return pl.BlockSpec(shape, lambda *_: (0,) * len(shape), pipeline_mode=pl.Buffered(1))


def _params(n_axes):
    return pltpu.CompilerParams(dimension_semantics=("parallel",) * n_axes,
                                vmem_limit_bytes=V7X_VMEM_LIMIT_BYTES)


def _macaron_half_step(x, g, win_ref, wout_ref):
    h = _rms(x, g).astype(BF16)
    gu = jnp.dot(h, win_ref[...], preferred_element_type=F32)
    gate = gu[:, :D_FF]
    up = gu[:, D_FF:]
    act = (gate * jax.nn.sigmoid(gate) * up).astype(BF16)
    return x + 0.5 * jnp.dot(act, wout_ref[...], preferred_element_type=F32)


def _sub_tiles(n_rows):
    return [slice(r, r + SUB_TILE_ROWS) for r in range(0, n_rows, SUB_TILE_ROWS)]


def _ffn_body(x_ref, g_ref, win_ref, wout_ref, o_ref):
    for rows in _sub_tiles(x_ref.shape[0]):
        o_ref[rows, :] = _macaron_half_step(x_ref[rows, :], g_ref[...], win_ref, wout_ref)


def _ffn(x, g, w_in, w_out):
    n = x.shape[0]
    row = pl.BlockSpec((WIDE_ROW_TILE, D_MODEL), lambda i: (i, 0))
    return pl.pallas_call(
        _ffn_body,
        grid=(n // WIDE_ROW_TILE,),
        in_specs=[row, _const_spec((1, D_MODEL)), _const_spec((D_MODEL, 2 * D_FF)),
                  _const_spec((D_FF, D_MODEL))],
        out_specs=row,
        out_shape=jax.ShapeDtypeStruct((n, D_MODEL), F32),
        compiler_params=_params(1),
        name="ffn",
    )(x, g, w_in, w_out)


def _proj_body(x_ref, *refs):
    n_outs = 5
    params, outs = refs[:-n_outs], refs[-n_outs:]
    a_ref, q_ref, k_ref, v_ref, gate_ref = outs
    for rows in _sub_tiles(x_ref.shape[0]):
        _proj_rows(x_ref.at[rows], *params, a_ref.at[rows], q_ref.at[:, rows], k_ref.at[:, rows],
                   v_ref.at[:, rows], gate_ref.at[rows])


def _proj_rows(x_ref, g_ref, win_ref, gbias_ref, sgug_ref, wcat_ref, sbias_ref, seg_ref,
               qg_ref, kg_ref, a_ref, q_ref, k_ref, v_ref, gate_ref):
    tm = x_ref.shape[0]
    h = _rms(x_ref[...], g_ref[...]).astype(BF16)
    proj = jnp.dot(h, win_ref[...], preferred_element_type=F32)
    c0 = 2 * A_WIDTH
    c1 = c0 + B_QK
    c2 = c1 + B_QK
    c3 = c2 + B_V

    uv = jax.nn.gelu(proj[:, :c0])
    u = uv[:, :A_WIDTH]
    vn = _rms(uv[:, A_WIDTH:], sgug_ref[...])
    lane = lax.broadcasted_iota(jnp.int32, (tm, A_WIDTH), 1)
    low = (lane & (V7X_LANES - 1)) < A_GROUP_DIM
    v_low = jnp.where(low, vn, 0.0).astype(BF16)
    v_high = jnp.where(low, 0.0, vn).astype(BF16)
    for c in range(0, tm // CHUNK, 2):
        pair = [slice((c + i) * CHUNK, (c + i + 1) * CHUNK) for i in range(2)]
        slabs = []
        for p in range(A_WIDTH // V7X_LANES):
            cols = slice(p * V7X_LANES, (p + 1) * V7X_LANES)
            rhs = jnp.concatenate(
                [jnp.concatenate([v_low[rows, cols], v_high[rows, cols]], axis=0)
                 for rows in pair], axis=1)
            slabs.append(jnp.dot(wcat_ref[p], rhs, preferred_element_type=F32))
        for i, rows in enumerate(pair):
            s = jnp.concatenate([slab[:, i * V7X_LANES:(i + 1) * V7X_LANES] for slab in slabs],
                                axis=1) + sbias_ref[...]
            a_ref[rows, :] = (u[rows, :] * s).astype(a_ref.dtype)

    def seg_norm(t, gain):
        sq = (t * t).astype(BF16)
        half = seg_ref.shape[0]
        ms = jnp.concatenate(
            [jnp.dot(sq[:, i * half:(i + 1) * half], seg_ref[...], preferred_element_type=F32)
             for i in range(B_QK // half)], axis=1)
        return (t * lax.rsqrt(ms + EPS) * gain).astype(BF16)

    for ref, val in ((q_ref, seg_norm(proj[:, c0:c1], qg_ref[...])),
                     (k_ref, seg_norm(proj[:, c1:c2], kg_ref[...])),
                     (v_ref, proj[:, c2:c3].astype(BF16))):
        for hd in range(B_HEADS):
            ref[hd] = val[:, hd * HEAD_WIDTH:(hd + 1) * HEAD_WIDTH]
    gate_ref[...] = jax.nn.sigmoid(proj[:, c3:] + gbias_ref[...]).astype(gate_ref.dtype)


def _proj(x, g, w_in, gate_bias, sgu_g, wcat, sbias, seg, qg, kg):
    n = x.shape[0]
    row = lambda w: pl.BlockSpec((WIDE_ROW_TILE, w), lambda i: (i, 0))
    heads = pl.BlockSpec((B_HEADS, WIDE_ROW_TILE, HEAD_WIDTH), lambda i: (0, i, 0))
    head_major = jax.ShapeDtypeStruct((B_HEADS, n, HEAD_WIDTH), BF16)
    flat = lambda w: jax.ShapeDtypeStruct((n, w), BF16)
    return pl.pallas_call(
        _proj_body,
        grid=(n // WIDE_ROW_TILE,),
        in_specs=[row(D_MODEL), _const_spec((1, D_MODEL)), _const_spec((D_MODEL, IN_COLS)),
                  _const_spec((1, 2 * D_MODEL)), _const_spec((1, A_WIDTH)),
                  _const_spec(wcat.shape), _const_spec(sbias.shape), _const_spec(seg.shape),
                  _const_spec((1, B_QK)), _const_spec((1, B_QK))],
        out_specs=[row(A_WIDTH), heads, heads, heads, row(2 * D_MODEL)],
        out_shape=[flat(A_WIDTH), head_major, head_major, head_major, flat(2 * D_MODEL)],
        compiler_params=_params(1),
        name="proj",
    )(x, g, w_in, gate_bias, sgu_g, wcat, sbias, seg, qg, kg)


def _bucket_thresholds():
    half = N_BUCKETS // 2
    max_exact = half // 2
    n_log = half - max_exact
    ratio = MAX_DISTANCE // max_exact
    out = []
    for j in range(1, n_log):
        target = max_exact ** n_log * ratio ** j
        n = max_exact
        while n ** n_log < target:
            n += 1
        out.append(n)
    return max_exact, out


def _prep_body(rb_ref, shift_ref, lq1_ref, lk1_ref, lq2_ref, lk2_ref, tiles_ref, tiles_t_ref,
               lam_ref, *, lambda_init):
    t = ATTN_TILE
    half = N_BUCKETS // 2
    max_exact, thresholds = _bucket_thresholds()
    row = lax.broadcasted_iota(jnp.int32, (t, t), 0)
    col = lax.broadcasted_iota(jnp.int32, (t, t), 1)
    shift = shift_ref[0, 0]
    for hd in range(B_HEADS):
        for idx, b in ((0, half - 1), (N_BIAS_TILES - 1, N_BUCKETS - 1)):
            far = jnp.full((t, t), rb_ref[b, hd] * LOG2E - shift, F32)
            tiles_ref[hd, idx] = far
            tiles_t_ref[hd, idx] = far
    for d in (-1, 0, 1):
        rel = col - row + d * t
        n = jnp.abs(rel)
        large = jnp.full((t, t), max_exact, jnp.int32)
        for th in thresholds:
            large = large + jnp.where(n >= th, 1, 0)
        bucket = jnp.where(rel > 0, half, 0) + jnp.where(n < max_exact, n, large)
        for hd in range(B_HEADS):
            tile = jnp.zeros((t, t), F32)
            for b in range(N_BUCKETS):
                tile = jnp.where(bucket == b, rb_ref[b, hd] * LOG2E - shift, tile)
            tiles_ref[hd, d + N_BIAS_TILES // 2] = tile
            tiles_t_ref[hd, d + N_BIAS_TILES // 2] = tile.T
    s1 = jnp.sum(lq1_ref[...] * lk1_ref[...], axis=-1, keepdims=True)
    s2 = jnp.sum(lq2_ref[...] * lk2_ref[...], axis=-1, keepdims=True)
    lam = jnp.exp(s1) - jnp.exp(s2) + lambda_init
    lam_ref[...] = jnp.broadcast_to(lam, lam_ref.shape)


def _prep(rel_bias, shift, lq1, lk1, lq2, lk2, lambda_init):
    t = ATTN_TILE
    vec = pl.BlockSpec((1, HEAD_DIM), lambda: (0, 0))
    return pl.pallas_call(
        functools.partial(_prep_body, lambda_init=lambda_init),
        in_specs=[pl.BlockSpec(memory_space=pltpu.SMEM), pl.BlockSpec(memory_space=pltpu.SMEM),
                  vec, vec, vec, vec],
        out_specs=[pl.BlockSpec((B_HEADS, N_BIAS_TILES, t, t), lambda: (0, 0, 0, 0)),
                   pl.BlockSpec((B_HEADS, N_BIAS_TILES, t, t), lambda: (0, 0, 0, 0)),
                   pl.BlockSpec((8, V7X_LANES), lambda: (0, 0))],
        out_shape=[jax.ShapeDtypeStruct((B_HEADS, N_BIAS_TILES, t, t), F32),
                   jax.ShapeDtypeStruct((B_HEADS, N_BIAS_TILES, t, t), F32),
                   jax.ShapeDtypeStruct((8, V7X_LANES), F32)],
        name="prep",
    )(rel_bias, shift, lq1, lk1, lq2, lk2)


def _q_stack(q_ref, c):
    t = ATTN_TILE
    lane = lax.broadcasted_iota(jnp.int32, (t, HEAD_WIDTH), 1)
    first = jnp.where(lane < HEAD_DIM, 1.0, 0.0).astype(BF16)
    q = q_ref[pl.ds(pl.multiple_of(c * t, t), t), :]
    return jnp.concatenate([q * first, q * (1.0 - first)], axis=0)


def _biased_logits(q2, c, j, k_ref, tiles_ref):
    t = ATTN_TILE
    ks = ATTN_KEY_STEP
    bias = jnp.concatenate(
        [tiles_ref[jnp.clip(jt - c, -2, 2) + N_BIAS_TILES // 2]
         for jt in range(j * ks // t, (j + 1) * ks // t)], axis=1)
    s = lax.dot_general(q2, k_ref[j * ks:(j + 1) * ks, :], (((1,), (1,)), ((), ())),
                        preferred_element_type=F32)
    return s + jnp.concatenate([bias, bias], axis=0)


def _store_output(res, c, lam_ref, subg_ref, o_ref, out_scale):
    t = ATTN_TILE
    out = res[:t] - lam_ref[0:1, 0:1] * res[t:]
    o_ref[pl.ds(pl.multiple_of(c * t, t), t), :] = (
        _rms(out, subg_ref[...]) * out_scale).astype(o_ref.dtype)


def _attn_exact_body(lam_ref, q_ref, k_ref, v_ref, tiles_ref, subg_ref, o_ref,
                     s_ref, m_ref, mnext_ref, l_ref, acc_ref, *, out_scale):
    t = ATTN_TILE
    ks = ATTN_KEY_STEP
    nk = k_ref.shape[0] // ks
    nq = q_ref.shape[0] // t
    reps = ks // V7X_LANES

    def logits_step(c, q2, j):
        s = _biased_logits(q2, c, j, k_ref, tiles_ref)
        s_ref[:, j * ks:(j + 1) * ks] = s
        blockmax = s[:, :V7X_LANES]
        for r in range(1, reps):
            blockmax = jnp.maximum(blockmax, s[:, r * V7X_LANES:(r + 1) * V7X_LANES])
        if j == 0:
            mnext_ref[...] = blockmax
        else:
            mnext_ref[...] = jnp.maximum(mnext_ref[...], blockmax)

    def finish_logits():
        m = jnp.max(mnext_ref[...], axis=1, keepdims=True)
        m_ref[...] = jnp.broadcast_to(m, m_ref.shape)

    def prob_step(j, s):
        p = jnp.exp2(s - jnp.concatenate([m_ref[...]] * reps, axis=1))
        psum = p[:, :V7X_LANES]
        for r in range(1, reps):
            psum = psum + p[:, r * V7X_LANES:(r + 1) * V7X_LANES]
        pv = jnp.dot(p.astype(BF16), v_ref[j * ks:(j + 1) * ks, :], preferred_element_type=F32)
        if j == 0:
            l_ref[...] = psum
            acc_ref[...] = pv
        else:
            l_ref[...] += psum
            acc_ref[...] += pv

    def finish_block(c):
        l = jnp.sum(l_ref[...], axis=1, keepdims=True)
        _store_output(acc_ref[...] / l, c, lam_ref, subg_ref, o_ref, out_scale)

    q2 = _q_stack(q_ref, 0)
    for j in range(nk):
        logits_step(0, q2, j)
    finish_logits()

    def body(c, carry):
        q2n = _q_stack(q_ref, c + 1)
        for j in range(nk):
            s = s_ref[:, j * ks:(j + 1) * ks]
            logits_step(c + 1, q2n, j)
            prob_step(j, s)
        finish_block(c)
        finish_logits()
        return carry

    lax.fori_loop(0, nq - 1, body, 0)

    for j in range(nk):
        prob_step(j, s_ref[:, j * ks:(j + 1) * ks])
    finish_block(nq - 1)


def _attn_fast_body(lam_ref, q_ref, k_ref, v_ref, tiles_t_ref, subg_ref, o_ref, pt_ref, vt_ref,
                    l_ref, *, out_scale):
    t = ATTN_TILE
    ks = ATTN_KEY_STEP
    seq = k_ref.shape[0]
    nk = seq // ks
    nq = q_ref.shape[0] // t

    vt_ref[...] = v_ref[...].astype(F32).T.astype(BF16)

    def prob_block(c):
        q2 = _q_stack(q_ref, c)
        total = None
        for j in range(nk):
            keys = slice(j * ks, (j + 1) * ks)
            bias = jnp.concatenate(
                [tiles_t_ref[jnp.clip(jt - c, -2, 2) + N_BIAS_TILES // 2]
                 for jt in range(j * ks // t, (j + 1) * ks // t)], axis=0)
            s = lax.dot_general(k_ref[keys, :], q2, (((1,), (1,)), ((), ())),
                                preferred_element_type=F32)
            p = jnp.exp2(s + jnp.concatenate([bias, bias], axis=1))
            part = jnp.sum(p, axis=0, keepdims=True)
            total = part if total is None else total + part
            pt_ref[keys, :] = p.astype(BF16)
        l_ref[...] = jnp.broadcast_to(total, l_ref.shape)

    def output_block(c):
        res = jnp.dot(vt_ref[...], pt_ref[...], preferred_element_type=F32)
        res = res / l_ref[0:1, :]
        out = res[:, :t] - lam_ref[0:1, 0:1] * res[:, t:]
        out = out * lax.rsqrt(jnp.mean(out * out, axis=0, keepdims=True) + EPS)
        o_ref[pl.ds(pl.multiple_of(c * t, t), t), :] = (
            out.T * subg_ref[...] * out_scale).astype(o_ref.dtype)

    prob_block(0)

    def body(c, carry):
        output_block(c - 1)
        prob_block(c)
        return carry

    lax.fori_loop(1, nq, body, 0, unroll=4)
    output_block(nq - 1)


def _attn(lam, q, k, v, tiles, tiles_t, subg, *, out_scale, single_pass):
    _, b, s, _ = q.shape
    t = ATTN_TILE
    seq = pl.BlockSpec((None, None, s, HEAD_WIDTH), lambda bi, hi: (hi, bi, 0, 0))
    out_seq = pl.BlockSpec((None, s, HEAD_WIDTH), lambda bi, hi: (bi, 0, hi))
    stack = pltpu.VMEM((2 * t, V7X_LANES), F32)
    if single_pass:
        body, name, bias_tiles = _attn_fast_body, "attn_fast", tiles_t
        scratch = [pltpu.VMEM((s, 2 * t), BF16), pltpu.VMEM((HEAD_WIDTH, s), BF16),
                   pltpu.VMEM((8, 2 * t), F32)]
    else:
        body, name, bias_tiles = _attn_exact_body, "attn_exact", tiles
        scratch = [pltpu.VMEM((2 * t, s), F32), stack, stack, stack, stack]
    return pl.pallas_call(
        functools.partial(body, out_scale=out_scale),
        grid=(b, B_HEADS),
        in_specs=[pl.BlockSpec((8, V7X_LANES), lambda bi, hi: (0, 0)),
                  seq, seq, seq,
                  pl.BlockSpec((None, N_BIAS_TILES, t, t), lambda bi, hi: (hi, 0, 0, 0)),
                  pl.BlockSpec((1, HEAD_WIDTH), lambda bi, hi: (0, 0))],
        out_specs=out_seq,
        out_shape=jax.ShapeDtypeStruct((b, s, B_V), BF16),
        scratch_shapes=scratch,
        compiler_params=_params(2),
        name=name,
    )(lam, q, k, v, bias_tiles, subg)


def _logit_shift(q_gain, k_gain, rel_bias):
    qk_bound = 1.02 * HEAD_DIM * jnp.max(jnp.abs(q_gain)) * jnp.max(jnp.abs(k_gain))
    bias = rel_bias.astype(F32) * LOG2E
    shift = qk_bound + jnp.max(bias)
    span = 2.0 * qk_bound + jnp.max(bias) - jnp.min(bias)
    return shift.reshape(1, 1), span <= LOGIT_SPAN


def _merge_ffn_body(x_ref, a_ref, b_ref, gate_ref, wpa_ref, wpb_ref, wo_ref, g_ref, win_ref,
                    wout_ref, fg_ref, o_ref):
    ga = gate_ref[:, :D_MODEL].astype(F32)
    gb = gate_ref[:, D_MODEL:].astype(F32)
    merged = (ga * jnp.dot(a_ref[...], wpa_ref[...], preferred_element_type=F32)
              + gb * jnp.dot(b_ref[...], wpb_ref[...], preferred_element_type=F32))
    x = x_ref[...] + jnp.dot(merged.astype(BF16), wo_ref[...], preferred_element_type=F32)
    for rows in _sub_tiles(x.shape[0]):
        o_ref[rows, :] = _rms(_macaron_half_step(x[rows, :], g_ref[...], win_ref, wout_ref),
                              fg_ref[...])


def _merge_ffn(x, a, b, gates, wpa, wpb, wo, g, w_in, w_out, final_g):
    n = x.shape[0]
    row = lambda w: pl.BlockSpec((ROW_TILE, w), lambda i: (i, 0))
    return pl.pallas_call(
        _merge_ffn_body,
        grid=(n // ROW_TILE,),
        in_specs=[row(D_MODEL), row(A_WIDTH), row(B_V), row(2 * D_MODEL),
                  _const_spec((A_WIDTH, D_MODEL)), _const_spec((B_V, D_MODEL)),
                  _const_spec((D_MODEL, D_MODEL)), _const_spec((1, D_MODEL)),
                  _const_spec((D_MODEL, 2 * D_FF)), _const_spec((D_FF, D_MODEL)),
                  _const_spec((1, D_MODEL))],
        out_specs=row(D_MODEL),
        out_shape=jax.ShapeDtypeStruct((n, D_MODEL), F32),
        compiler_params=_params(1),
        name="merge_ffn",
    )(x, a, b, gates, wpa, wpb, wo, g, w_in, w_out, final_g)


def _layer(x, lp, tiles, tiles_t, lam, out_scale, single_pass):
    b, s, _ = x.shape
    x0 = x.reshape(b * s, D_MODEL)
    x1 = _ffn(x0, lp["ffn1_norm"], lp["ffn1_w_in"], lp["ffn1_w_out"])
    a, q, k, v, gates = _proj(x1, lp["mix_norm"], lp["w_in"], lp["gate_bias"], lp["sgu_norm"],
                              lp["wcat"], lp["sbias"], lp["seg"], lp["qg"], lp["kg"])
    bo = lax.cond(single_pass,
                  functools.partial(_attn, out_scale=out_scale, single_pass=True),
                  functools.partial(_attn, out_scale=out_scale, single_pass=False),
                  lam, *(t.reshape(B_HEADS, b, s, HEAD_WIDTH) for t in (q, k, v)),
                  tiles, tiles_t, lp["diff_subln"])
    y = _merge_ffn(x1, a, bo.reshape(b * s, B_V), gates, lp["w_proj_a"], lp["w_proj_b"], lp["w_out"],
                   lp["ffn2_norm"], lp["ffn2_w_in"], lp["ffn2_w_out"], lp["final_norm"])
    return y.reshape(b, s, D_MODEL)


def kernel(x_prompt, x_sample, rel_bias, ffn1_norm, ffn1_w_in, ffn1_w_out, mix_norm, w_in, gate_bias, sgu_norm, sgu_w, sgu_b, q_norm, k_norm, lambda_q1, lambda_k1, lambda_q2, lambda_k2, diff_subln, w_proj_a, w_proj_b, w_out, ffn2_norm, ffn2_w_in, ffn2_w_out, final_norm):
    depth = ffn1_norm.shape[0]
    y_prompt, y_sample = x_prompt, x_sample
    row = lambda p: p.reshape(1, -1).astype(F32)
    for l in range(depth):
        lambda_init = 0.8 - 0.6 * math.exp(-0.3 * l)
        sw = sgu_w[l].astype(BF16)
        lp = {
            "ffn1_norm": row(ffn1_norm[l]), "ffn1_w_in": ffn1_w_in[l].astype(BF16),
            "ffn1_w_out": ffn1_w_out[l].astype(BF16),
            "mix_norm": row(mix_norm[l]), "w_in": w_in[l].astype(BF16),
            "gate_bias": row(gate_bias[l]), "sgu_norm": row(sgu_norm[l]),
            "wcat": jnp.concatenate([sw[0::2], sw[1::2]], axis=-1),
            "sbias": jnp.repeat(sgu_b[l].T.astype(F32), A_GROUP_DIM, axis=1),
            "seg": jnp.kron(jnp.eye(V7X_MXU_DIM // HEAD_DIM, dtype=F32),
                            jnp.full((HEAD_DIM, HEAD_DIM), 1.0 / HEAD_DIM, F32)).astype(BF16),
            "qg": jnp.tile(row(q_norm[l]) * (HEAD_DIM ** -0.5 * LOG2E), (1, B_QK // HEAD_DIM)),
            "kg": jnp.tile(row(k_norm[l]), (1, B_QK // HEAD_DIM)),
            "diff_subln": row(diff_subln[l]),
            "w_proj_a": w_proj_a[l].astype(BF16), "w_proj_b": w_proj_b[l].astype(BF16),
            "w_out": w_out[l].astype(BF16),
            "ffn2_norm": row(ffn2_norm[l]), "ffn2_w_in": ffn2_w_in[l].astype(BF16),
            "ffn2_w_out": ffn2_w_out[l].astype(BF16), "final_norm": row(final_norm[l]),
        }
        shift, single_pass = _logit_shift(lp["qg"], lp["kg"], rel_bias)
        tiles, tiles_t, lam = _prep(rel_bias.astype(F32), shift, row(lambda_q1[l]), row(lambda_k1[l]),
                                    row(lambda_q2[l]), row(lambda_k2[l]), lambda_init)
        out_scale = 1.0 - lambda_init
        y_prompt = _layer(y_prompt, lp, tiles, tiles_t, lam, out_scale, single_pass)
        y_sample = _layer(y_sample, lp, tiles, tiles_t, lam, out_scale, single_pass)
    return (y_prompt, y_sample)
```
